```python
import jax
import jax.numpy as jnp
from jax import lax
import numpy as np

D_MODEL = 2048
BATCH = 1
SEQ = 8192
DEPTH = 2
DEC_BATCH = 128
DEC_SEQ = 8
PAST_LEN = 2048
PAGE_SIZE = 128

N_EVEN = (DEPTH + 1) // 2
N_ODD = DEPTH // 2
ALPHA = (2.0 * DEPTH) ** 0.25
BETA = (8.0 * DEPTH) ** -0.25
LN_EPS = 1e-5
W_LRU = D_MODEL // 2
LRU_CONV = 4
LRU_BLOCKS = 8
LRU_BLOCK = W_LRU // LRU_BLOCKS
LRU_C = 8.0
NSA_HQ = 8
NSA_HKV = 2
NSA_HD = 128
NSA_G = NSA_HQ // NSA_HKV
NSA_KVW = NSA_HKV * NSA_HD
CMP_STRIDE = 16
CMP_LEN = 2 * CMP_STRIDE
SEL_BLOCK = 64
SEL_TOPK = 16
WINDOW = 512
Q_BLOCK = 128
ROPE_THETA = 10000.0
BIG = 1e6
NEG = -1e30
W_MIX0 = W_LRU + NSA_HQ * NSA_HD
EVEN_SPLITS = [W_LRU, 2 * W_LRU, 2 * W_LRU + NSA_HQ * NSA_HD, 2 * W_LRU + NSA_HQ * NSA_HD + 4 * NSA_KVW, 2 * W_LRU + NSA_HQ * NSA_HD + 6 * NSA_KVW]
IN0 = EVEN_SPLITS[-1] + 3 * NSA_HQ
SC_CONV = 3
N_MEM = 256
MEM_HEADS = 4
MEM_HD = D_MODEL // MEM_HEADS
D_FF = ((8 * D_MODEL + 3 * 256 - 1) // (3 * 256)) * 256

kernel_name = 'hybrid_rglru_nsa_shortconv_decoder_step'


def layer_norm(x, g, b):
    xf = x.astype(jnp.float32)
    mu = jnp.mean(xf, axis=-1, keepdims=True)
    var = jnp.mean(jnp.square(xf - mu), axis=-1, keepdims=True)
    return ((xf - mu) * lax.rsqrt(var + LN_EPS) * g + b).astype(x.dtype)


def post_norm(x, sub, g, b):
    return layer_norm(ALPHA * x + sub, g, b)


def rope(x, pos):
    half = x.shape[-1] // 2
    inv = ROPE_THETA ** (-jnp.arange(half, dtype=jnp.float32) / half)
    ang = pos.astype(jnp.float32)[:, None] * inv[None, :]
    cos = jnp.cos(ang)[:, None, :]
    sin = jnp.sin(ang)[:, None, :]
    xf = x.astype(jnp.float32)
    x1, x2 = xf[..., :half], xf[..., half:]
    return jnp.concatenate([x1 * cos - x2 * sin, x2 * cos + x1 * sin], axis=-1).astype(x.dtype)


def masked_softmax(s, mask):
    s = jnp.where(mask, s.astype(jnp.float32), NEG)
    e = jnp.exp(s - jnp.max(s, axis=-1, keepdims=True)) * mask
    return e / jnp.maximum(jnp.sum(e, axis=-1, keepdims=True), 1e-30)


def causal_depthwise_conv(x, buf, w):
    t = x.shape[1]
    xp = jnp.concatenate([buf.astype(x.dtype), x], axis=1)
    y = sum(w[i] * xp[:, i:i + t] for i in range(w.shape[0]))
    return y, xp[:, t:]


def block_diag(x, w):
    b, t, width = x.shape
    nb, bs, _ = w.shape
    return jnp.einsum('btnc,ncd->btnd', x.reshape(b, t, nb, bs), w).reshape(b, t, width)


def rglru(xb, gb, conv_buf, h0, conv_w, conv_b, wx, bx, wa, ba, lam):
    xc, new_buf = causal_depthwise_conv(xb, conv_buf, conv_w)
    xc = xc + conv_b
    i_g = jax.nn.sigmoid(block_diag(xc, wx) + bx)
    r_g = jax.nn.sigmoid(block_diag(xc, wa) + ba)
    log_a = -LRU_C * r_g.astype(jnp.float32) * jax.nn.softplus(-lam.astype(jnp.float32))
    a = jnp.exp(log_a)
    u = jnp.sqrt(-jnp.expm1(2.0 * log_a)) * (i_g * xc).astype(jnp.float32)
    u = u.at[:, 0].add(a[:, 0] * h0.astype(jnp.float32))

    def combine(l, r):
        return (l[0] * r[0], r[0] * l[1] + r[1])

    _, h = lax.associative_scan(combine, (a, u), axis=1)
    y = h.astype(xb.dtype) * jax.nn.gelu(gb)
    return y, new_buf, h[:, -1].astype(h0.dtype)


def compress_blocks(k):
    b, t, h, d = k.shape
    n_ch = -(-t // CMP_STRIDE)
    kp = jnp.pad(k, ((0, 0), (0, n_ch * CMP_STRIDE - t), (0, 0), (0, 0)))
    ch = jnp.sum(kp.astype(jnp.float32).reshape(b, n_ch, CMP_STRIDE, h, d), axis=2)
    return ((ch[:, :-1] + ch[:, 1:]) / CMP_LEN).astype(k.dtype)


def sel_blocks(k):
    b, t, h, d = k.shape
    n_sel = -(-t // SEL_BLOCK)
    kp = jnp.pad(k, ((0, 0), (0, n_sel * SEL_BLOCK - t), (0, 0), (0, 0)))
    return kp.reshape(b, n_sel, SEL_BLOCK, h, d)


def nsa_core(q, g, pos_q, kc, vc, kb, vb, kw, vw, pos_w):
    b, tq = q.shape[:2]
    scale = NSA_HD ** -0.5
    qg = q.reshape(b, tq, NSA_HKV, NSA_G, NSA_HD)
    n_cmp = kc.shape[1]
    cmp_end = jnp.arange(n_cmp, dtype=jnp.int32) * CMP_STRIDE + (CMP_LEN - 1)
    s = jnp.einsum('bqhgd,bkhd->bhgqk', qg, kc) * scale
    p_cmp = masked_softmax(s, cmp_end[None, :] <= pos_q[:, None])
    o_cmp = jnp.einsum('bhgqk,bkhd->bqhgd', p_cmp.astype(vc.dtype), vc)
    imp = jnp.sum(p_cmp, axis=2)
    pad3 = ((0, 0), (0, 0), (0, 0))
    chunk = jnp.pad(imp, pad3 + ((0, 1),)) + jnp.pad(imp, pad3 + ((1, 0),))
    n_sel = kb.shape[1]
    per = SEL_BLOCK // CMP_STRIDE
    chunk = jnp.pad(chunk, pad3 + ((0, n_sel * per - n_cmp - 1),))
    imp_sel = jnp.sum(chunk.reshape(b, NSA_HKV, tq, n_sel, per), axis=-1)
    blk = jnp.arange(n_sel, dtype=jnp.int32)[None, :]
    cur = (pos_q // SEL_BLOCK)[:, None]
    forced = (blk == 0) | (blk == cur) | (blk == cur - 1)
    valid = blk * SEL_BLOCK <= pos_q[:, None]
    score = jnp.where(valid, jnp.where(forced, BIG, imp_sel), -BIG)
    _, idx = lax.top_k(score, min(SEL_TOPK, n_sel))
    gather = jax.vmap(jax.vmap(lambda a, i: a[i]))
    n_key = idx.shape[-1] * SEL_BLOCK
    ks = gather(jnp.transpose(kb, (0, 3, 1, 2, 4)), idx).reshape(b, NSA_HKV, tq, n_key, NSA_HD)
    vs = gather(jnp.transpose(vb, (0, 3, 1, 2, 4)), idx).reshape(b, NSA_HKV, tq, n_key, NSA_HD)
    key_pos = (idx[..., None] * SEL_BLOCK + jnp.arange(SEL_BLOCK, dtype=jnp.int32)).reshape(b, NSA_HKV, tq, n_key)
    s = jnp.einsum('bqhgd,bhqkd->bhgqk', qg, ks) * scale
    p = masked_softmax(s, (key_pos <= pos_q[:, None])[:, :, None])
    o_sel = jnp.einsum('bhgqk,bhqkd->bqhgd', p.astype(vs.dtype), vs)
    s = jnp.einsum('bqhgd,bkhd->bhgqk', qg, kw) * scale
    dpos = pos_q[:, None] - pos_w[None, :]
    p = masked_softmax(s, (dpos >= 0) & (dpos < WINDOW) & (pos_w[None, :] >= 0))
    o_win = jnp.einsum('bhgqk,bkhd->bqhgd', p.astype(vw.dtype), vw)
    gg = g.reshape(b, tq, NSA_HKV, NSA_G, 3, 1)
    o = gg[..., 0, :] * o_cmp + gg[..., 1, :] * o_sel + gg[..., 2, :] * o_win
    return o.reshape(b, tq, NSA_HQ * NSA_HD)


def nsa_prompt(q, kv4, kvw, gates):
    b, t = q.shape[:2]
    kc, vc = compress_blocks(kv4[:, :, 0]), compress_blocks(kv4[:, :, 1])
    kb, vb = sel_blocks(kv4[:, :, 2]), sel_blocks(kv4[:, :, 3])
    kvw_pad = jnp.pad(kvw, ((0, 0), (WINDOW, 0), (0, 0), (0, 0), (0, 0)))
    n_qb = t // Q_BLOCK
    qb = jnp.swapaxes(q.reshape(b, n_qb, Q_BLOCK, NSA_HQ, NSA_HD), 0, 1)
    gb = jnp.swapaxes(gates.reshape(b, n_qb, Q_BLOCK, NSA_HQ, 3), 0, 1)
    span = jnp.arange(WINDOW + Q_BLOCK, dtype=jnp.int32)

    def one_block(args):
        i, q_i, g_i = args
        s0 = i * Q_BLOCK
        pos_q = s0 + jnp.arange(Q_BLOCK, dtype=jnp.int32)
        kw_i = lax.dynamic_slice_in_dim(kvw_pad, s0, WINDOW + Q_BLOCK, axis=1)
        return nsa_core(q_i, g_i, pos_q, kc, vc, kb, vb, kw_i[:, :, 0], kw_i[:, :, 1], s0 - WINDOW + span)

    o = lax.map(one_block, (jnp.arange(n_qb, dtype=jnp.int32), qb, gb))
    return jnp.swapaxes(o, 0, 1).reshape(b, t, NSA_HQ * NSA_HD)


def nsa_sample(q, kv4, kvw, gates, pool, page_table, win_buf):
    tq = q.shape[1]
    past_len = page_table.shape[1] * PAGE_SIZE
    w_buf = win_buf.shape[1]
    pos_q = past_len + jnp.arange(tq, dtype=jnp.int32)
    pos_w = past_len - w_buf + jnp.arange(w_buf + tq, dtype=jnp.int32)
    kvw_all = jnp.concatenate([win_buf.astype(kvw.dtype), kvw], axis=1)

    def one_seq(args):
        q_i, g_i, pt_i, kv4_i, kvw_i = args
        past = pool[pt_i].reshape(past_len, 4, NSA_HKV, NSA_HD)
        full = jnp.concatenate([past.astype(kv4_i.dtype), kv4_i], axis=0)[None]
        kc, vc = compress_blocks(full[:, :, 0]), compress_blocks(full[:, :, 1])
        kb, vb = sel_blocks(full[:, :, 2]), sel_blocks(full[:, :, 3])
        o = nsa_core(q_i[None], g_i[None], pos_q, kc, vc, kb, vb, kvw_i[None, :, 0], kvw_i[None, :, 1], pos_w)
        return o[0]

    o = lax.map(one_seq, (q, gates, page_table, kv4, kvw_all))
    return o, kvw_all[:, tq:]


def even_project(x, pos, w_in):
    b, t, _ = x.shape
    xb, gb, q, kv4, kvw, gt = jnp.split(x @ w_in, EVEN_SPLITS, axis=-1)
    q = rope(q.reshape(b, t, NSA_HQ, NSA_HD), pos)
    kv4 = kv4.reshape(b, t, 4, NSA_HKV, NSA_HD)
    kv4 = jnp.stack([rope(kv4[:, :, 0], pos), kv4[:, :, 1], rope(kv4[:, :, 2], pos), kv4[:, :, 3]], axis=2)
    kvw = kvw.reshape(b, t, 2, NSA_HKV, NSA_HD)
    kvw = jnp.stack([rope(kvw[:, :, 0], pos), kvw[:, :, 1]], axis=2)
    gates = jax.nn.sigmoid(gt.reshape(b, t, NSA_HQ, 3))
    return xb, gb, q, kv4, kvw, gates


def even_mixer_prompt(x, w_in, w_out, lp):
    b, t, _ = x.shape
    pos = jnp.arange(t, dtype=jnp.int32)
    xb, gb, q, kv4, kvw, gates = even_project(x, pos, w_in)
    conv0 = jnp.zeros((b, LRU_CONV - 1, W_LRU), x.dtype)
    h0 = jnp.zeros((b, W_LRU), x.dtype)
    y_lru, new_conv, h_last = rglru(xb, gb, conv0, h0, *lp)
    o_nsa = nsa_prompt(q, kv4, kvw, gates)
    y = jnp.concatenate([y_lru, o_nsa], axis=-1) @ w_out
    return y, kv4, kvw[:, t - min(WINDOW, t):], new_conv, h_last


def even_mixer_sample(x, pos, w_in, w_out, lp, conv_buf, h0, pool, page_table, win_buf):
    xb, gb, q, kv4, kvw, gates = even_project(x, pos, w_in)
    y_lru, new_conv, h_last = rglru(xb, gb, conv_buf, h0, *lp)
    o_nsa, new_win = nsa_sample(q, kv4, kvw, gates, pool, page_table, win_buf)
    y = jnp.concatenate([y_lru, o_nsa], axis=-1) @ w_out
    return y, kv4, new_win, new_conv, h_last


def short_conv_mixer(x, buf, w_in, conv_w, w_out):
    bg, cg, h = jnp.split(x @ w_in, 3, axis=-1)
    y, new_buf = causal_depthwise_conv(cg * h, buf, conv_w)
    return (bg * y) @ w_out, new_buf


def mem_kv_proj(mem, w_kv):
    b, n, _ = mem.shape
    return (mem @ w_kv).reshape(b, n, 2, MEM_HEADS, MEM_HD)


def cross_attn(x, kv, w_q, w_o):
    b, t, _ = x.shape
    q = (x @ w_q).reshape(b, t, MEM_HEADS, MEM_HD)
    s = jnp.einsum('bqhd,bkhd->bhqk', q, kv[:, :, 0].astype(x.dtype)).astype(jnp.float32) * (MEM_HD ** -0.5)
    p = jax.nn.softmax(s, axis=-1).astype(x.dtype)
    o = jnp.einsum('bhqk,bkhd->bqhd', p, kv[:, :, 1].astype(x.dtype)).reshape(b, t, MEM_HEADS * MEM_HD)
    return o @ w_o


def swiglu(x, w_gu, w_down):
    g, u = jnp.split(x @ w_gu, 2, axis=-1)
    return (jax.nn.silu(g) * u) @ w_down


def setup_inputs(seed: int = 0) -> dict:
    key = jax.random.key(seed)
    ks = jax.random.split(key, 30)

    def nrm(i, shape, scale=1.0):
        return jax.random.normal(ks[i], shape, jnp.float32) * scale

    n_pages = PAST_LEN // PAGE_SIZE
    n_pool = (DEC_BATCH * n_pages * 5 + 3) // 4
    w_buf = min(WINDOW, PAST_LEN)
    page_table = jax.random.permutation(ks[9], n_pool)[:DEC_BATCH * n_pages].reshape(DEC_BATCH, n_pages).astype(jnp.int32)
    a8 = jax.random.uniform(ks[17], (N_EVEN, W_LRU), jnp.float32, 0.9, 0.999)
    a_base = a8 ** (1.0 / LRU_C)
    return {
        'x_prompt': nrm(0, (BATCH, SEQ, D_MODEL)),
        'x_sample': nrm(1, (DEC_BATCH, DEC_SEQ, D_MODEL)),
        'cache_nsa_kv': nrm(3, (N_EVEN, n_pool, PAGE_SIZE, 4, NSA_HKV, NSA_HD)),
        'state_nsa_win_kv': nrm(4, (N_EVEN, DEC_BATCH, w_buf, 2, NSA_HKV, NSA_HD)),
        'state_lru_conv': nrm(5, (N_EVEN, DEC_BATCH, LRU_CONV - 1, W_LRU)),
        'state_lru_h': nrm(6, (N_EVEN, DEC_BATCH, W_LRU), 0.5),
        'state_sconv': nrm(7, (N_ODD, DEC_BATCH, SC_CONV - 1, D_MODEL)),
        'cache_mem_kv': nrm(8, (DEPTH, DEC_BATCH, N_MEM, 2, MEM_HEADS, MEM_HD)),
        'page_table': page_table,
        'mem_prompt': nrm(2, (BATCH, N_MEM, D_MODEL)),
        'w_in0': nrm(10, (N_EVEN, D_MODEL, IN0), D_MODEL ** -0.5),
        'lru_conv_w': nrm(11, (N_EVEN, LRU_CONV, W_LRU), LRU_CONV ** -0.5),
        'lru_conv_b': nrm(12, (N_EVEN, W_LRU), 0.01),
        'lru_wx': nrm(13, (N_EVEN, LRU_BLOCKS, LRU_BLOCK, LRU_BLOCK), LRU_BLOCK ** -0.5),
        'lru_bx': nrm(14, (N_EVEN, W_LRU), 0.01),
        'lru_wa': nrm(15, (N_EVEN, LRU_BLOCKS, LRU_BLOCK, LRU_BLOCK), LRU_BLOCK ** -0.5),
        'lru_ba': nrm(16, (N_EVEN, W_LRU), 0.01),
        'lru_lambda': jnp.log(a_base) - jnp.log1p(-a_base),
        'w_out0': nrm(18, (N_EVEN, W_MIX0, D_MODEL), BETA * W_MIX0 ** -0.5),
        'w_in1': nrm(19, (N_ODD, D_MODEL, 3 * D_MODEL), D_MODEL ** -0.5),
        'sconv_w': nrm(20, (N_ODD, SC_CONV, D_MODEL), SC_CONV ** -0.5),
        'w_out1': nrm(21, (N_ODD, D_MODEL, D_MODEL), BETA * D_MODEL ** -0.5),
        'w_q_mem': nrm(22, (DEPTH, D_MODEL, MEM_HEADS * MEM_HD), D_MODEL ** -0.5),
        'w_kv_mem': nrm(23, (DEPTH, D_MODEL, 2 * MEM_HEADS * MEM_HD), D_MODEL ** -0.5),
        'w_o_mem': nrm(24, (DEPTH, MEM_HEADS * MEM_HD, D_MODEL), BETA * (MEM_HEADS * MEM_HD) ** -0.5),
        'w_gu': nrm(25, (DEPTH, D_MODEL, 2 * D_FF), D_MODEL ** -0.5),
        'w_down': nrm(26, (DEPTH, D_FF, D_MODEL), BETA * D_FF ** -0.5),
        'ln_g': 1.0 + nrm(27, (DEPTH, 3, D_MODEL), 0.02),
        'ln_b': nrm(28, (DEPTH, 3, D_MODEL), 0.02),
    }


def reference(x_prompt, x_sample, cache_nsa_kv, state_nsa_win_kv, state_lru_conv, state_lru_h, state_sconv, cache_mem_kv, page_table, mem_prompt, w_in0, lru_conv_w, lru_conv_b, lru_wx, lru_bx, lru_wa, lru_ba, lru_lambda, w_out0, w_in1, sconv_w, w_out1, w_q_mem, w_kv_mem, w_o_mem, w_gu, w_down, ln_g, ln_b):
    past_len = page_table.shape[1] * PAGE_SIZE
    pos_s = past_len + jnp.arange(x_sample.shape[1], dtype=jnp.int32)
    xp, xs = x_prompt, x_sample
    p_nsa, p_win, p_conv, p_h, p_sc, p_mem = [], [], [], [], [], []
    s_nsa, s_win, s_conv, s_h, s_sc = [], [], [], [], []
    for l in range(DEPTH):
        if l % 2 == 0:
            e = l // 2
            lp = (lru_conv_w[e], lru_conv_b[e], lru_wx[e], lru_bx[e], lru_wa[e], lru_ba[e], lru_lambda[e])
            mp, kv4_p, win_p, conv_p, h_p = even_mixer_prompt(xp, w_in0[e], w_out0[e], lp)
            ms, kv4_s, win_s, conv_s, h_s = even_mixer_sample(xs, pos_s, w_in0[e], w_out0[e], lp, state_lru_conv[e], state_lru_h[e], cache_nsa_kv[e], page_table, state_nsa_win_kv[e])
            p_nsa.append(kv4_p)
            p_win.append(win_p)
            p_conv.append(conv_p)
            p_h.append(h_p)
            s_nsa.append(kv4_s)
            s_win.append(win_s)
            s_conv.append(conv_s)
            s_h.append(h_s)
        else:
            o = l // 2
            buf0 = jnp.zeros((xp.shape[0], SC_CONV - 1, D_MODEL), xp.dtype)
            mp, sc_p = short_conv_mixer(xp, buf0, w_in1[o], sconv_w[o], w_out1[o])
            ms, sc_s = short_conv_mixer(xs, state_sconv[o], w_in1[o], sconv_w[o], w_out1[o])
            p_sc.append(sc_p)
            s_sc.append(sc_s)
        xp = post_norm(xp, mp, ln_g[l, 0], ln_b[l, 0])
        xs = post_norm(xs, ms, ln_g[l, 0], ln_b[l, 0])
        kv_mem_p = mem_kv_proj(mem_prompt, w_kv_mem[l])
        p_mem.append(kv_mem_p)
        xp = post_norm(xp, cross_attn(xp, kv_mem_p, w_q_mem[l], w_o_mem[l]), ln_g[l, 1], ln_b[l, 1])
        xs = post_norm(xs, cross_attn(xs, cache_mem_kv[l], w_q_mem[l], w_o_mem[l]), ln_g[l, 1], ln_b[l, 1])
        xp = post_norm(xp, swiglu(xp, w_gu[l], w_down[l]), ln_g[l, 2], ln_b[l, 2])
        xs = post_norm(xs, swiglu(xs, w_gu[l], w_down[l]), ln_g[l, 2], ln_b[l, 2])
    return (xp, xs, jnp.stack(p_nsa), jnp.stack(p_win), jnp.stack(p_conv), jnp.stack(p_h), jnp.stack(p_sc), jnp.stack(p_mem), jnp.stack(s_nsa), jnp.stack(s_win), jnp.stack(s_conv), jnp.stack(s_h), jnp.stack(s_sc))
```

```python
import functools

import jax
import jax.numpy as jnp
from jax import lax
from jax.experimental import pallas as pl
from jax.experimental.pallas import tpu as pltpu

D_MODEL = 2048
SEQ = 8192
DEPTH = 2
DEC_BATCH = 128
DEC_SEQ = 8
PAGE_SIZE = 128

ALPHA = (2.0 * DEPTH) ** 0.25
LN_EPS = 1e-5
W_LRU = D_MODEL // 2
LRU_CONV = 4
LRU_BLOCKS = 8
LRU_BLOCK = W_LRU // LRU_BLOCKS
LRU_C = 8.0
NSA_HQ = 8
NSA_HKV = 2
NSA_HD = 128
NSA_G = NSA_HQ // NSA_HKV
NSA_KVW = NSA_HKV * NSA_HD
CMP_STRIDE = 16
CMP_LEN = 2 * CMP_STRIDE
SEL_BLOCK = 64
SEL_TOPK = 16
WINDOW = 512
Q_BLOCK = 128
ROPE_THETA = 10000.0
BIG = 1e6
NEG = -1e30
EVEN_SPLITS = [W_LRU, 2 * W_LRU, 2 * W_LRU + NSA_HQ * NSA_HD,
               2 * W_LRU + NSA_HQ * NSA_HD + 4 * NSA_KVW,
               2 * W_LRU + NSA_HQ * NSA_HD + 6 * NSA_KVW]
N_MAIN0 = EVEN_SPLITS[-1]
N_GATE = 3 * NSA_HQ
SC_CONV = 3
N_MEM = 256
MEM_HEADS = 4
MEM_HD = D_MODEL // MEM_HEADS
D_FF = ((8 * D_MODEL + 3 * 256 - 1) // (3 * 256)) * 256

LANE = 128
VMEM_LIMIT = 56 * 1024 * 1024

F32 = jnp.float32
BF16 = jnp.bfloat16


def _params(sem):
    return pltpu.CompilerParams(dimension_semantics=sem, vmem_limit_bytes=VMEM_LIMIT)


def _mm_kernel(x_ref, w_ref, o_ref, *, nk):
    part = jnp.dot(x_ref[...].astype(BF16), w_ref[...].astype(BF16),
                   preferred_element_type=F32)
    if nk == 1:
        o_ref[...] = part
    else:
        k = pl.program_id(2)

        @pl.when(k == 0)
        def _():
            o_ref[...] = part

        @pl.when(k > 0)
        def _():
            o_ref[...] += part


def matmul(x, w, *, n_cols=None, tm=1024, tn=512, tk=None):
    m, kdim = x.shape
    n = w.shape[1] if n_cols is None else n_cols
    tm = min(tm, m)
    tn = min(tn, n)
    tk = kdim if tk is None else tk
    assert m % tm == 0 and n % tn == 0 and kdim % tk == 0
    nk = kdim // tk
    return pl.pallas_call(
        functools.partial(_mm_kernel, nk=nk),
        grid=(m // tm, n // tn, nk),
        in_specs=[pl.BlockSpec((tm, tk), lambda i, j, k: (i, k)),
                  pl.BlockSpec((tk, tn), lambda i, j, k: (k, j))],
        out_specs=pl.BlockSpec((tm, tn), lambda i, j, k: (i, j)),
        out_shape=jax.ShapeDtypeStruct((m, n), F32),
        compiler_params=_params(("parallel", "parallel", "arbitrary")),
    )(x, w)


def _mm_ln_kernel(x_ref, w_ref, res_ref, g_ref, b_ref, o_ref, acc_ref, *, nk):
    k = pl.program_id(1)
    part = jnp.dot(x_ref[...].astype(BF16), w_ref[...].astype(BF16),
                   preferred_element_type=F32)

    @pl.when(k == 0)
    def _():
        acc_ref[...] = part

    @pl.when(k > 0)
    def _():
        acc_ref[...] += part

    @pl.when(k == nk - 1)
    def _():
        z = ALPHA * res_ref[...] + acc_ref[...]
        mu = jnp.mean(z, axis=-1, keepdims=True)
        zc = z - mu
        var = jnp.mean(zc * zc, axis=-1, keepdims=True)
        o_ref[...] = zc * lax.rsqrt(var + LN_EPS) * g_ref[...] + b_ref[...]


def matmul_postnorm(x, w, res, g, b, *, tm=512, tk=512):
    m, kdim = x.shape
    n = w.shape[1]
    assert m % tm == 0 and kdim % tk == 0 and res.shape == (m, n)
    nk = kdim // tk
    return pl.pallas_call(
        functools.partial(_mm_ln_kernel, nk=nk),
        grid=(m // tm, nk),
        in_specs=[pl.BlockSpec((tm, tk), lambda i, k: (i, k)),
                  pl.BlockSpec((tk, n), lambda i, k: (k, 0)),
                  pl.BlockSpec((tm, n), lambda i, k: (i, 0)),
                  pl.BlockSpec((1, n), lambda i, k: (0, 0)),
                  pl.BlockSpec((1, n), lambda i, k: (0, 0))],
        out_specs=pl.BlockSpec((tm, n), lambda i, k: (i, 0)),
        out_shape=jax.ShapeDtypeStruct((m, n), F32),
        scratch_shapes=[pltpu.VMEM((tm, n), F32)],
        compiler_params=_params(("parallel", "arbitrary")),
    )(x, w, res, g.reshape(1, n), b.reshape(1, n))


def _mm_swiglu_kernel(x_ref, wg_ref, wu_ref, o_ref):
    xb = x_ref[...].astype(BF16)
    g = jnp.dot(xb, wg_ref[...].astype(BF16), preferred_element_type=F32)
    u = jnp.dot(xb, wu_ref[...].astype(BF16), preferred_element_type=F32)
    o_ref[...] = g * jax.nn.sigmoid(g) * u


def matmul_swiglu(x, w_gu, *, tm=512, tn=512):
    m, kdim = x.shape
    f = w_gu.shape[1] // 2
    assert m % tm == 0 and f % tn == 0
    nj = f // tn
    return pl.pallas_call(
        _mm_swiglu_kernel,
        grid=(m // tm, nj),
        in_specs=[pl.BlockSpec((tm, kdim), lambda i, j: (i, 0)),
                  pl.BlockSpec((kdim, tn), lambda i, j: (0, j)),
                  pl.BlockSpec((kdim, tn), lambda i, j: (0, j + nj))],
        out_specs=pl.BlockSpec((tm, tn), lambda i, j: (i, j)),
        out_shape=jax.ShapeDtypeStruct((m, f), F32),
        compiler_params=_params(("parallel", "parallel")),
    )(x, w_gu, w_gu)


def rope(x, pos):
    half = x.shape[-1] // 2
    inv = ROPE_THETA ** (-jnp.arange(half, dtype=F32) / half)
    ang = pos.astype(F32)[:, None] * inv[None, :]
    cos = jnp.cos(ang)[:, None, :]
    sin = jnp.sin(ang)[:, None, :]
    x1, x2 = x[..., :half], x[..., half:]
    return jnp.concatenate([x1 * cos - x2 * sin, x2 * cos + x1 * sin], axis=-1)


def masked_softmax(s, mask):
    s = jnp.where(mask, s, NEG)
    e = jnp.exp(s - jnp.max(s, axis=-1, keepdims=True)) * mask
    return e / jnp.maximum(jnp.sum(e, axis=-1, keepdims=True), 1e-30)


def causal_depthwise_conv(x, buf, w):
    t = x.shape[1]
    xp = jnp.concatenate([buf, x], axis=1)
    y = sum(w[i] * xp[:, i:i + t] for i in range(w.shape[0]))
    return y, xp[:, t:]


def block_diag(x, w):
    b, t, width = x.shape
    nb, bs, _ = w.shape
    return jnp.einsum('btnc,ncd->btnd', x.reshape(b, t, nb, bs), w).reshape(b, t, width)


def rglru(xb, gb, conv_buf, h0, conv_w, conv_b, wx, bx, wa, ba, lam):
    xc, new_buf = causal_depthwise_conv(xb, conv_buf, conv_w)
    xc = xc + conv_b
    i_g = jax.nn.sigmoid(block_diag(xc, wx) + bx)
    r_g = jax.nn.sigmoid(block_diag(xc, wa) + ba)
    log_a = -LRU_C * r_g * jax.nn.softplus(-lam)
    a = jnp.exp(log_a)
    u = jnp.sqrt(-jnp.expm1(2.0 * log_a)) * (i_g * xc)
    u = u.at[:, 0].add(a[:, 0] * h0)

    def combine(l, r):
        return (l[0] * r[0], r[0] * l[1] + r[1])

    _, h = lax.associative_scan(combine, (a, u), axis=1)
    y = h * jax.nn.gelu(gb)
    return y, new_buf, h[:, -1]


def compress_blocks(k):
    b, t, h, d = k.shape
    n_ch = -(-t // CMP_STRIDE)
    kp = jnp.pad(k, ((0, 0), (0, n_ch * CMP_STRIDE - t), (0, 0), (0, 0)))
    ch = jnp.sum(kp.reshape(b, n_ch, CMP_STRIDE, h, d), axis=2)
    return (ch[:, :-1] + ch[:, 1:]) / CMP_LEN


def sel_blocks(k):
    b, t, h, d = k.shape
    n_sel = -(-t // SEL_BLOCK)
    kp = jnp.pad(k, ((0, 0), (0, n_sel * SEL_BLOCK - t), (0, 0), (0, 0)))
    return kp.reshape(b, n_sel, SEL_BLOCK, h, d)


def nsa_core(q, g, pos_q, kc, vc, kb, vb, kw, vw, pos_w):
    b, tq = q.shape[:2]
    scale = NSA_HD ** -0.5
    qg = q.reshape(b, tq, NSA_HKV, NSA_G, NSA_HD)
    n_cmp = kc.shape[1]
    cmp_end = jnp.arange(n_cmp, dtype=jnp.int32) * CMP_STRIDE + (CMP_LEN - 1)
    s = jnp.einsum('bqhgd,bkhd->bhgqk', qg, kc) * scale
    p_cmp = masked_softmax(s, cmp_end[None, :] <= pos_q[:, None])
    o_cmp = jnp.einsum('bhgqk,bkhd->bqhgd', p_cmp, vc)
    imp = jnp.sum(p_cmp, axis=2)
    pad3 = ((0, 0), (0, 0), (0, 0))
    chunk = jnp.pad(imp, pad3 + ((0, 1),)) + jnp.pad(imp, pad3 + ((1, 0),))
    n_sel = kb.shape[1]
    per = SEL_BLOCK // CMP_STRIDE
    chunk = jnp.pad(chunk, pad3 + ((0, n_sel * per - n_cmp - 1),))
    imp_sel = jnp.sum(chunk.reshape(b, NSA_HKV, tq, n_sel, per), axis=-1)
    blk = jnp.arange(n_sel, dtype=jnp.int32)[None, :]
    cur = (pos_q // SEL_BLOCK)[:, None]
    forced = (blk == 0) | (blk == cur) | (blk == cur - 1)
    valid = blk * SEL_BLOCK <= pos_q[:, None]
    score = jnp.where(valid, jnp.where(forced, BIG, imp_sel), -BIG)
    _, idx = lax.top_k(score, min(SEL_TOPK, n_sel))
    gather = jax.vmap(jax.vmap(lambda a, i: a[i]))
    n_key = idx.shape[-1] * SEL_BLOCK
    ks = gather(jnp.transpose(kb, (0, 3, 1, 2, 4)), idx).reshape(b, NSA_HKV, tq, n_key, NSA_HD)
    vs = gather(jnp.transpose(vb, (0, 3, 1, 2, 4)), idx).reshape(b, NSA_HKV, tq, n_key, NSA_HD)
    key_pos = (idx[..., None] * SEL_BLOCK + jnp.arange(SEL_BLOCK, dtype=jnp.int32)).reshape(b, NSA_HKV, tq, n_key)
    s = jnp.einsum('bqhgd,bhqkd->bhgqk', qg, ks) * scale
    p = masked_softmax(s, (key_pos <= pos_q[:, None])[:, :, None])
    o_sel = jnp.einsum('bhgqk,bhqkd->bqhgd', p, vs)
    s = jnp.einsum('bqhgd,bkhd->bhgqk', qg, kw) * scale
    dpos = pos_q[:, None] - pos_w[None, :]
    p = masked_softmax(s, (dpos >= 0) & (dpos < WINDOW) & (pos_w[None, :] >= 0))
    o_win = jnp.einsum('bhgqk,bkhd->bqhgd', p, vw)
    gg = g.reshape(b, tq, NSA_HKV, NSA_G, 3, 1)
    o = gg[..., 0, :] * o_cmp + gg[..., 1, :] * o_sel + gg[..., 2, :] * o_win
    return o.reshape(b, tq, NSA_HQ * NSA_HD)


def nsa_prompt(q, kv4, kvw, gates):
    b, t = q.shape[:2]
    kc, vc = compress_blocks(kv4[:, :, 0]), compress_blocks(kv4[:, :, 1])
    kb, vb = sel_blocks(kv4[:, :, 2]), sel_blocks(kv4[:, :, 3])
    kvw_pad = jnp.pad(kvw, ((0, 0), (WINDOW, 0), (0, 0), (0, 0), (0, 0)))
    n_qb = t // Q_BLOCK
    qb = jnp.swapaxes(q.reshape(b, n_qb, Q_BLOCK, NSA_HQ, NSA_HD), 0, 1)
    gb = jnp.swapaxes(gates.reshape(b, n_qb, Q_BLOCK, NSA_HQ, 3), 0, 1)
    span = jnp.arange(WINDOW + Q_BLOCK, dtype=jnp.int32)

    def one_block(args):
        i, q_i, g_i = args
        s0 = i * Q_BLOCK
        pos_q = s0 + jnp.arange(Q_BLOCK, dtype=jnp.int32)
        kw_i = lax.dynamic_slice_in_dim(kvw_pad, s0, WINDOW + Q_BLOCK, axis=1)
        return nsa_core(q_i, g_i, pos_q, kc, vc, kb, vb, kw_i[:, :, 0], kw_i[:, :, 1], s0 - WINDOW + span)

    o = lax.map(one_block, (jnp.arange(n_qb, dtype=jnp.int32), qb, gb))
    return jnp.swapaxes(o, 0, 1).reshape(b, t, NSA_HQ * NSA_HD)


def nsa_sample(q, kv4, kvw, gates, pool, page_table, win_buf):
    tq = q.shape[1]
    past_len = page_table.shape[1] * PAGE_SIZE
    w_buf = win_buf.shape[1]
    pos_q = past_len + jnp.arange(tq, dtype=jnp.int32)
    pos_w = past_len - w_buf + jnp.arange(w_buf + tq, dtype=jnp.int32)
    kvw_all = jnp.concatenate([win_buf, kvw], axis=1)

    def one_seq(args):
        q_i, g_i, pt_i, kv4_i, kvw_i = args
        past = pool[pt_i].reshape(past_len, 4, NSA_HKV, NSA_HD)
        full = jnp.concatenate([past, kv4_i], axis=0)[None]
        kc, vc = compress_blocks(full[:, :, 0]), compress_blocks(full[:, :, 1])
        kb, vb = sel_blocks(full[:, :, 2]), sel_blocks(full[:, :, 3])
        o = nsa_core(q_i[None], g_i[None], pos_q, kc, vc, kb, vb, kvw_i[None, :, 0], kvw_i[None, :, 1], pos_w)
        return o[0]

    o = lax.map(one_seq, (q, gates, page_table, kv4, kvw_all))
    return o, kvw_all[:, tq:]


def split_even(proj, gt, b, t, pos):
    xb = proj[:, :EVEN_SPLITS[0]].reshape(b, t, W_LRU)
    gb = proj[:, EVEN_SPLITS[0]:EVEN_SPLITS[1]].reshape(b, t, W_LRU)
    q = rope(proj[:, EVEN_SPLITS[1]:EVEN_SPLITS[2]].reshape(b, t, NSA_HQ, NSA_HD), pos)
    kv4 = proj[:, EVEN_SPLITS[2]:EVEN_SPLITS[3]].reshape(b, t, 4, NSA_HKV, NSA_HD)
    kv4 = jnp.stack([rope(kv4[:, :, 0], pos), kv4[:, :, 1], rope(kv4[:, :, 2], pos), kv4[:, :, 3]], axis=2)
    kvw = proj[:, EVEN_SPLITS[3]:EVEN_SPLITS[4]].reshape(b, t, 2, NSA_HKV, NSA_HD)
    kvw = jnp.stack([rope(kvw[:, :, 0], pos), kvw[:, :, 1]], axis=2)
    gates = jax.nn.sigmoid(gt.reshape(b, t, NSA_HQ, 3))
    return xb, gb, q, kv4, kvw, gates


def cross_attn_core(q, kv):
    b, t, _ = q.shape
    q = q.reshape(b, t, MEM_HEADS, MEM_HD)
    s = jnp.einsum('bqhd,bkhd->bhqk', q, kv[:, :, 0]) * (MEM_HD ** -0.5)
    p = jax.nn.softmax(s, axis=-1)
    return jnp.einsum('bhqk,bkhd->bqhd', p, kv[:, :, 1]).reshape(b, t, MEM_HEADS * MEM_HD)


def kernel(x_prompt, x_sample, cache_nsa_kv, state_nsa_win_kv, state_lru_conv, state_lru_h, state_sconv, cache_mem_kv, page_table, mem_prompt, w_in0, lru_conv_w, lru_conv_b, lru_wx, lru_bx, lru_wa, lru_ba, lru_lambda, w_out0, w_in1, sconv_w, w_out1, w_q_mem, w_kv_mem, w_o_mem, w_gu, w_down, ln_g, ln_b):
    bp, tp = x_prompt.shape[:2]
    bs, ts = x_sample.shape[:2]
    mp, ms = bp * tp, bs * ts
    past_len = page_table.shape[1] * PAGE_SIZE
    pos_p = jnp.arange(tp, dtype=jnp.int32)
    pos_s = past_len + jnp.arange(ts, dtype=jnp.int32)
    xa = jnp.concatenate([x_prompt.reshape(mp, D_MODEL), x_sample.reshape(ms, D_MODEL)], axis=0)
    p_nsa, p_win, p_conv, p_h, p_sc, p_mem = [], [], [], [], [], []
    s_nsa, s_win, s_conv, s_h, s_sc = [], [], [], [], []
    for l in range(DEPTH):
        if l % 2 == 0:
            e = l // 2
            lp = (lru_conv_w[e], lru_conv_b[e], lru_wx[e], lru_bx[e], lru_wa[e], lru_ba[e], lru_lambda[e])
            proj = matmul(xa, w_in0[e], n_cols=N_MAIN0)
            w_gate = jnp.pad(w_in0[e][:, N_MAIN0:], ((0, 0), (0, LANE - N_GATE)))
            gt = matmul(xa, w_gate)[:, :N_GATE]
            xb, gb, q, kv4, kvw, gates = split_even(proj[:mp], gt[:mp], bp, tp, pos_p)
            conv0 = jnp.zeros((bp, LRU_CONV - 1, W_LRU), F32)
            h0 = jnp.zeros((bp, W_LRU), F32)
            y_lru, conv_p, h_p = rglru(xb, gb, conv0, h0, *lp)
            o_nsa = nsa_prompt(q, kv4, kvw, gates)
            mix_p = jnp.concatenate([y_lru, o_nsa], axis=-1).reshape(mp, D_MODEL)
            p_nsa.append(kv4)
            p_win.append(kvw[:, tp - min(WINDOW, tp):])
            p_conv.append(conv_p)
            p_h.append(h_p)
            xb, gb, q, kv4, kvw, gates = split_even(proj[mp:], gt[mp:], bs, ts, pos_s)
            y_lru, conv_s, h_s = rglru(xb, gb, state_lru_conv[e], state_lru_h[e], *lp)
            o_nsa, win_s = nsa_sample(q, kv4, kvw, gates, cache_nsa_kv[e], page_table, state_nsa_win_kv[e])
            mix_s = jnp.concatenate([y_lru, o_nsa], axis=-1).reshape(ms, D_MODEL)
            s_nsa.append(kv4)
            s_win.append(win_s)
            s_conv.append(conv_s)
            s_h.append(h_s)
            mix = jnp.concatenate([mix_p, mix_s], axis=0)
            w_out = w_out0[e]
        else:
            o = l // 2
            h3 = matmul(xa, w_in1[o])
            mixes = []
            for rows, b, t, buf, acc in ((h3[:mp], bp, tp, jnp.zeros((bp, SC_CONV - 1, D_MODEL), F32), p_sc),
                                         (h3[mp:], bs, ts, state_sconv[o], s_sc)):
                bg, cg, hh = jnp.split(rows.reshape(b, t, 3 * D_MODEL), 3, axis=-1)
                y, new_buf = causal_depthwise_conv(cg * hh, buf, sconv_w[o])
                acc.append(new_buf)
                mixes.append((bg * y).reshape(b * t, D_MODEL))
            mix = jnp.concatenate(mixes, axis=0)
            w_out = w_out1[o]
        xa = matmul_postnorm(mix, w_out, xa, ln_g[l, 0], ln_b[l, 0])
        kv_mem_p = matmul(mem_prompt.reshape(bp * N_MEM, D_MODEL), w_kv_mem[l]).reshape(bp, N_MEM, 2, MEM_HEADS, MEM_HD)
        p_mem.append(kv_mem_p)
        qm = matmul(xa, w_q_mem[l])
        om = jnp.concatenate([cross_attn_core(qm[:mp].reshape(bp, tp, -1), kv_mem_p).reshape(mp, -1),
                              cross_attn_core(qm[mp:].reshape(bs, ts, -1), cache_mem_kv[l]).reshape(ms, -1)], axis=0)
        xa = matmul_postnorm(om, w_o_mem[l], xa, ln_g[l, 1], ln_b[l, 1])
        hf = matmul_swiglu(xa, w_gu[l])
        xa = matmul_postnorm(hf, w_down[l], xa, ln_g[l, 2], ln_b[l, 2])
    return (xa[:mp].reshape(bp, tp, D_MODEL), xa[mp:].reshape(bs, ts, D_MODEL),
            jnp.stack(p_nsa), jnp.stack(p_win), jnp.stack(p_conv), jnp.stack(p_h), jnp.stack(p_sc),
            jnp.stack(p_mem), jnp.stack(s_nsa), jnp.stack(s_win), jnp.stack(s_conv), jnp.stack(s_h),
            jnp.stack(s_sc))
```

```python
import functools

import numpy as np
import jax
import jax.numpy as jnp
from jax import lax
from jax.experimental import pallas as pl
from jax.experimental.pallas import tpu as pltpu

D_MODEL = 2048
SEQ = 8192
DEPTH = 2
DEC_BATCH = 128
DEC_SEQ = 8
PAGE_SIZE = 128

ALPHA = (2.0 * DEPTH) ** 0.25
LN_EPS = 1e-5
W_LRU = D_MODEL // 2
LRU_CONV = 4
LRU_BLOCKS = 8
LRU_BLOCK = W_LRU // LRU_BLOCKS
LRU_C = 8.0
NSA_HQ = 8
NSA_HKV = 2
NSA_HD = 128
NSA_G = NSA_HQ // NSA_HKV
NSA_KVW = NSA_HKV * NSA_HD
CMP_STRIDE = 16
CMP_LEN = 2 * CMP_STRIDE
SEL_BLOCK = 64
SEL_TOPK = 16
WINDOW = 512
Q_BLOCK = 128
ROPE_THETA = 10000.0
BIG = 1e6
NEG = -1e30
NSA_SCALE = NSA_HD ** -0.5
EVEN_SPLITS = [W_LRU, 2 * W_LRU, 2 * W_LRU + NSA_HQ * NSA_HD,
               2 * W_LRU + NSA_HQ * NSA_HD + 4 * NSA_KVW,
               2 * W_LRU + NSA_HQ * NSA_HD + 6 * NSA_KVW]
N_MAIN0 = EVEN_SPLITS[-1]
N_GATE = 3 * NSA_HQ
SC_CONV = 3
N_MEM = 256
MEM_HEADS = 4
MEM_HD = D_MODEL // MEM_HEADS
D_FF = ((8 * D_MODEL + 3 * 256 - 1) // (3 * 256)) * 256

LANE = 128
VMEM_LIMIT = 56 * 1024 * 1024
SEL_TILE = 512

F32 = jnp.float32
BF16 = jnp.bfloat16
I32 = jnp.int32


def _params(sem):
    return pltpu.CompilerParams(dimension_semantics=sem, vmem_limit_bytes=VMEM_LIMIT)


def _dot(a, b):
    return jnp.dot(a, b, preferred_element_type=F32)


def _dot_nt(a, b):
    return lax.dot_general(a, b, (((1,), (1,)), ((), ())), preferred_element_type=F32)


def _mm_kernel(x_ref, w_ref, o_ref, *, nk):
    part = _dot(x_ref[...].astype(BF16), w_ref[...].astype(BF16))
    if nk == 1:
        o_ref[...] = part
    else:
        k = pl.program_id(2)

        @pl.when(k == 0)
        def _():
            o_ref[...] = part

        @pl.when(k > 0)
        def _():
            o_ref[...] += part


def matmul(x, w, *, n_cols=None, tm=1024, tn=512, tk=None):
    m, kdim = x.shape
    n = w.shape[1] if n_cols is None else n_cols
    tm = min(tm, m)
    tn = min(tn, n)
    tk = kdim if tk is None else tk
    assert m % tm == 0 and n % tn == 0 and kdim % tk == 0
    nk = kdim // tk
    return pl.pallas_call(
        functools.partial(_mm_kernel, nk=nk),
        grid=(m // tm, n // tn, nk),
        in_specs=[pl.BlockSpec((tm, tk), lambda i, j, k: (i, k)),
                  pl.BlockSpec((tk, tn), lambda i, j, k: (k, j))],
        out_specs=pl.BlockSpec((tm, tn), lambda i, j, k: (i, j)),
        out_shape=jax.ShapeDtypeStruct((m, n), F32),
        compiler_params=_params(("parallel", "parallel", "arbitrary")),
    )(x, w)


def _mm_ln_kernel(x_ref, w_ref, res_ref, g_ref, b_ref, o_ref, acc_ref, *, nk):
    k = pl.program_id(1)
    part = _dot(x_ref[...].astype(BF16), w_ref[...].astype(BF16))

    @pl.when(k == 0)
    def _():
        acc_ref[...] = part

    @pl.when(k > 0)
    def _():
        acc_ref[...] += part

    @pl.when(k == nk - 1)
    def _():
        z = ALPHA * res_ref[...] + acc_ref[...]
        mu = jnp.mean(z, axis=-1, keepdims=True)
        zc = z - mu
        var = jnp.mean(zc * zc, axis=-1, keepdims=True)
        o_ref[...] = zc * lax.rsqrt(var + LN_EPS) * g_ref[...] + b_ref[...]


def matmul_postnorm(x, w, res, g, b, *, tm=512, tk=512):
    m, kdim = x.shape
    n = w.shape[1]
    assert m % tm == 0 and kdim % tk == 0 and res.shape == (m, n)
    nk = kdim // tk
    return pl.pallas_call(
        functools.partial(_mm_ln_kernel, nk=nk),
        grid=(m // tm, nk),
        in_specs=[pl.BlockSpec((tm, tk), lambda i, k: (i, k)),
                  pl.BlockSpec((tk, n), lambda i, k: (k, 0)),
                  pl.BlockSpec((tm, n), lambda i, k: (i, 0)),
                  pl.BlockSpec((1, n), lambda i, k: (0, 0)),
                  pl.BlockSpec((1, n), lambda i, k: (0, 0))],
        out_specs=pl.BlockSpec((tm, n), lambda i, k: (i, 0)),
        out_shape=jax.ShapeDtypeStruct((m, n), F32),
        scratch_shapes=[pltpu.VMEM((tm, n), F32)],
        compiler_params=_params(("parallel", "arbitrary")),
    )(x, w, res, g.reshape(1, n), b.reshape(1, n))


def _mm_swiglu_kernel(x_ref, wg_ref, wu_ref, o_ref):
    xb = x_ref[...].astype(BF16)
    g = _dot(xb, wg_ref[...].astype(BF16))
    u = _dot(xb, wu_ref[...].astype(BF16))
    o_ref[...] = g * jax.nn.sigmoid(g) * u


def matmul_swiglu(x, w_gu, *, tm=512, tn=512):
    m, kdim = x.shape
    f = w_gu.shape[1] // 2
    assert m % tm == 0 and f % tn == 0
    nj = f // tn
    return pl.pallas_call(
        _mm_swiglu_kernel,
        grid=(m // tm, nj),
        in_specs=[pl.BlockSpec((tm, kdim), lambda i, j: (i, 0)),
                  pl.BlockSpec((kdim, tn), lambda i, j: (0, j)),
                  pl.BlockSpec((kdim, tn), lambda i, j: (0, j + nj))],
        out_specs=pl.BlockSpec((tm, tn), lambda i, j: (i, j)),
        out_shape=jax.ShapeDtypeStruct((m, f), F32),
        compiler_params=_params(("parallel", "parallel")),
    )(x, w_gu, w_gu)


def _rope_tables(pos):
    half = NSA_HD // 2
    inv = ROPE_THETA ** (-jnp.arange(half, dtype=F32) / half)
    ang = pos.astype(F32)[:, None] * inv[None, :]
    cos, sin = jnp.cos(ang), jnp.sin(ang)
    return jnp.concatenate([cos, cos], axis=-1), jnp.concatenate([-sin, sin], axis=-1)


def _nsa_prep_kernel(q_ref, kv_ref, kw_ref, gt_ref, cos_ref, sin_ref,
                     qbf_ref, kv_o_ref, kvbf_ref, kw_o_ref, kwbf_ref, g_o_ref, ch_ref):
    cos = cos_ref[...]
    sin = sin_ref[...]
    rows = q_ref.shape[0]

    def rot(x):
        return x * cos + pltpu.roll(x, NSA_HD // 2, axis=1) * sin

    for h in range(NSA_HQ):
        sl = slice(NSA_HD * h, NSA_HD * (h + 1))
        qbf_ref[:, sl] = rot(q_ref[:, sl]).astype(BF16)
    for c in range(4 * NSA_HKV):
        sl = slice(NSA_HD * c, NSA_HD * (c + 1))
        slot = c // NSA_HKV
        x = kv_ref[:, sl]
        if slot in (0, 2):
            x = rot(x)
        kv_o_ref[:, sl] = x
        kvbf_ref[:, sl] = x.astype(BF16)
        if slot < 2:
            ch_ref[:, sl] = jnp.sum(x.reshape(rows // CMP_STRIDE, CMP_STRIDE, NSA_HD), axis=1)
    for c in range(2 * NSA_HKV):
        sl = slice(NSA_HD * c, NSA_HD * (c + 1))
        x = kw_ref[:, sl]
        if c // NSA_HKV == 0:
            x = rot(x)
        kw_o_ref[:, sl] = x
        kwbf_ref[:, sl] = x.astype(BF16)
    g_o_ref[...] = jax.nn.sigmoid(gt_ref[...])


def nsa_prep(proj, gt, cos, sin, *, tm=512):
    m = proj.shape[0]
    assert m % tm == 0
    qc = EVEN_SPLITS[1] // (NSA_HQ * NSA_HD)
    kc = EVEN_SPLITS[2] // (4 * NSA_KVW)
    wc = EVEN_SPLITS[3] // (2 * NSA_KVW)
    nq, nkv, nkw, ng = NSA_HQ * NSA_HD, 4 * NSA_KVW, 2 * NSA_KVW, NSA_HKV * LANE
    row = lambda n: pl.BlockSpec((tm, n), lambda i: (i, 0))
    return pl.pallas_call(
        _nsa_prep_kernel,
        grid=(m // tm,),
        in_specs=[pl.BlockSpec((tm, nq), lambda i: (i, qc)),
                  pl.BlockSpec((tm, nkv), lambda i: (i, kc)),
                  pl.BlockSpec((tm, nkw), lambda i: (i, wc)),
                  row(ng), row(NSA_HD), row(NSA_HD)],
        out_specs=[row(nq), row(nkv), row(nkv), row(nkw), row(nkw), row(ng),
                   pl.BlockSpec((tm // CMP_STRIDE, 2 * NSA_KVW), lambda i: (i, 0))],
        out_shape=[jax.ShapeDtypeStruct((m, nq), BF16),
                   jax.ShapeDtypeStruct((m, nkv), F32),
                   jax.ShapeDtypeStruct((m, nkv), BF16),
                   jax.ShapeDtypeStruct((m, nkw), F32),
                   jax.ShapeDtypeStruct((m, nkw), BF16),
                   jax.ShapeDtypeStruct((m, ng), F32),
                   jax.ShapeDtypeStruct((m // CMP_STRIDE, 2 * NSA_KVW), F32)],
        compiler_params=_params(("parallel",)),
    )(proj, proj, proj, gt, cos, sin)


def _softmax_rows(s, mask):
    sm = jnp.where(mask, s, NEG)
    e = jnp.where(mask, jnp.exp(sm - jnp.max(sm, axis=-1, keepdims=True)), 0.0)
    return e / jnp.maximum(jnp.sum(e, axis=-1, keepdims=True), 1e-30)


def _block_importance(imp, pmat):
    hi = imp.astype(BF16)
    r1 = imp - hi.astype(F32)
    mid = r1.astype(BF16)
    lo = (r1 - mid.astype(F32)).astype(BF16)
    return _dot(hi, pmat) + _dot(mid, pmat) + _dot(lo, pmat)


def _select_blocks(imp_sel, posq):
    blk = lax.broadcasted_iota(I32, (1, LANE), 1)
    cur = posq // SEL_BLOCK
    forced = (blk == 0) | (blk == cur) | (blk == cur - 1)
    valid = blk * SEL_BLOCK <= posq
    score = jnp.where(valid, jnp.where(forced, BIG, imp_sel), -BIG)
    key = lax.bitcast_convert_type(score, I32)
    key = jnp.where(key < 0, key ^ 0x7FFFFFFF, key)
    key_m1 = key - 1
    cnt = jnp.zeros(score.shape, I32)
    for i in range(LANE):
        thr = jnp.where(blk > i, key_m1, key)
        cnt = cnt + (key[:, i:i + 1] > thr).astype(I32)
    return (cnt < SEL_TOPK).astype(F32)


def _pool_matrix(n_cmp):
    c = np.arange(n_cmp)[:, None]
    j = np.arange(LANE)[None, :]
    return jnp.asarray(((c // 4 == j).astype(np.float32) + ((c + 1) // 4 == j)), dtype=BF16)


def _nsa_prompt_kernel(q_ref, g_ref, chk_ref, chv_ref, ks_ref, vs_ref, kw_ref, vw_ref, pmat_ref,
                       o_ref, kct_ref, vc_ref):
    qb = pl.program_id(1)
    n_cmp = chk_ref.shape[0]

    @pl.when(qb == 0)
    def _():
        ck = chk_ref[...]
        kc = (ck + pltpu.roll(ck, n_cmp - 1, axis=0)) * (1.0 / CMP_LEN)
        kct_ref[...] = kc.T.astype(BF16)
        cv = chv_ref[...]
        vc_ref[...] = ((cv + pltpu.roll(cv, n_cmp - 1, axis=0)) * (1.0 / CMP_LEN)).astype(BF16)

    qblk = q_ref[...]
    q2 = jnp.concatenate([qblk[:, NSA_HD * g:NSA_HD * (g + 1)] for g in range(NSA_G)], axis=0)
    rows = NSA_G * Q_BLOCK
    posq = qb * Q_BLOCK + lax.broadcasted_iota(I32, (Q_BLOCK, 1), 0)
    posq4 = jnp.concatenate([posq] * NSA_G, axis=0)

    s = _dot(q2, kct_ref[...]) * NSA_SCALE
    cend = lax.broadcasted_iota(I32, (1, n_cmp), 1) * CMP_STRIDE + (CMP_LEN - 1)
    p = _softmax_rows(s, cend <= posq4)
    o_cmp = _dot(p.astype(BF16), vc_ref[...])
    imp = p[0:Q_BLOCK]
    for g in range(1, NSA_G):
        imp = imp + p[g * Q_BLOCK:(g + 1) * Q_BLOCK]
    sel = _select_blocks(_block_importance(imp, pmat_ref[...]), posq).astype(BF16)

    brow = lax.broadcasted_iota(I32, (LANE, 1), 0)
    kcol = lax.broadcasted_iota(I32, (1, SEL_TILE), 1)
    kdiv = kcol // SEL_BLOCK

    def body(t, carry):
        m, l, acc = carry
        off = pl.multiple_of(t * SEL_TILE, SEL_TILE)
        k = ks_ref[pl.ds(off, SEL_TILE), :]
        v = vs_ref[pl.ds(off, SEL_TILE), :]
        st = _dot_nt(q2, k) * NSA_SCALE
        expand = (brow == kdiv + t * (SEL_TILE // SEL_BLOCK)).astype(BF16)
        chosen = _dot(sel, expand)
        msk = (chosen > 0.5) & (kcol + t * SEL_TILE <= posq)
        sm = st.reshape(NSA_G, Q_BLOCK, SEL_TILE) + jnp.where(msk, 0.0, NEG)[None]
        m_new = jnp.maximum(m, jnp.max(sm, axis=-1, keepdims=True))
        alpha = jnp.exp(m - m_new)
        e = jnp.exp(sm - m_new)
        l = alpha * l + jnp.sum(e, axis=-1, keepdims=True)
        pv = _dot(e.reshape(rows, SEL_TILE).astype(BF16), v)
        acc = alpha.reshape(rows, 1) * acc + pv
        return m_new, l, acc

    m0 = jnp.full((NSA_G, Q_BLOCK, 1), NEG, F32)
    l0 = jnp.zeros((NSA_G, Q_BLOCK, 1), F32)
    n_tiles = qb // (SEL_TILE // Q_BLOCK) + 1
    _, l, acc = lax.fori_loop(0, n_tiles, body, (m0, l0, jnp.zeros((rows, NSA_HD), F32)))
    o_sel = acc / jnp.maximum(l.reshape(rows, 1), 1e-30)

    span = WINDOW + Q_BLOCK
    start = pl.multiple_of(jnp.maximum(qb - WINDOW // Q_BLOCK, 0) * Q_BLOCK, Q_BLOCK)
    kw = kw_ref[pl.ds(start, span), :]
    vw = vw_ref[pl.ds(start, span), :]
    s = _dot_nt(q2, kw) * NSA_SCALE
    dpos = posq4 - (start + lax.broadcasted_iota(I32, (1, span), 1))
    p = _softmax_rows(s, (dpos >= 0) & (dpos < WINDOW))
    o_win = _dot(p.astype(BF16), vw)

    gates = g_ref[...]
    for g in range(NSA_G):
        rs = slice(g * Q_BLOCK, (g + 1) * Q_BLOCK)
        o_ref[:, NSA_HD * g:NSA_HD * (g + 1)] = (gates[:, 3 * g:3 * g + 1] * o_cmp[rs]
                                                + gates[:, 3 * g + 1:3 * g + 2] * o_sel[rs]
                                                + gates[:, 3 * g + 2:3 * g + 3] * o_win[rs])


def nsa_prompt(q_bf, gates, ch, kv_bf, kw_bf, t):
    n_qb = t // Q_BLOCK
    n_cmp = t // CMP_STRIDE
    gw = NSA_G * NSA_HD
    res = lambda c0: pl.BlockSpec((t, NSA_HD), lambda h, i: (0, c0 + h))
    return pl.pallas_call(
        _nsa_prompt_kernel,
        grid=(NSA_HKV, n_qb),
        in_specs=[pl.BlockSpec((Q_BLOCK, gw), lambda h, i: (i, h)),
                  pl.BlockSpec((Q_BLOCK, LANE), lambda h, i: (i, h)),
                  pl.BlockSpec((n_cmp, NSA_HD), lambda h, i: (0, h)),
                  pl.BlockSpec((n_cmp, NSA_HD), lambda h, i: (0, NSA_HKV + h)),
                  res(2 * NSA_HKV), res(3 * NSA_HKV), res(0), res(NSA_HKV),
                  pl.BlockSpec((n_cmp, LANE), lambda h, i: (0, 0))],
        out_specs=pl.BlockSpec((Q_BLOCK, gw), lambda h, i: (i, h)),
        out_shape=jax.ShapeDtypeStruct((t, NSA_HQ * NSA_HD), F32),
        scratch_shapes=[pltpu.VMEM((NSA_HD, n_cmp), BF16), pltpu.VMEM((n_cmp, NSA_HD), BF16)],
        compiler_params=_params(("parallel", "arbitrary")),
    )(q_bf, gates, ch, ch, kv_bf, kv_bf, kw_bf, kw_bf, _pool_matrix(n_cmp))


def _nsa_sample_kernel(pt_ref, q_ref, g_ref, kvn_ref, kwn_ref, win_ref, *rest, n_pages, past_len):
    pages = rest[:n_pages]
    pmat_ref, emat_ref, o_ref, nwin_ref = rest[n_pages:]
    tq = q_ref.shape[1]
    rows = NSA_G * tq
    w_buf = win_ref.shape[1]
    pad_rows = PAGE_SIZE - tq
    qf = q_ref[0].astype(F32)
    kvn = kvn_ref[0]
    kwn = kwn_ref[0]
    gates = g_ref[0]
    posq = past_len + lax.broadcasted_iota(I32, (tq, 1), 0)
    posq4 = jnp.concatenate([posq] * NSA_G, axis=0)
    zpad = jnp.zeros((pad_rows, NSA_HD), F32)

    def new_page(x):
        return jnp.concatenate([x, zpad], axis=0).astype(BF16)

    def chunk_sums(x):
        return jnp.sum(x.reshape(x.shape[0] // CMP_STRIDE, CMP_STRIDE, NSA_HD), axis=1)

    n_cmp = past_len // CMP_STRIDE
    crow = lax.broadcasted_iota(I32, (n_cmp, 1), 0)

    def compressed(col):
        ch = jnp.concatenate([chunk_sums(pg[0, :, col:col + NSA_HD]) for pg in pages], axis=0)
        ch_new = jnp.sum(kvn[:, col:col + NSA_HD], axis=0, keepdims=True)
        nxt = jnp.where(crow == n_cmp - 1, ch_new, pltpu.roll(ch, n_cmp - 1, axis=0))
        return ((ch + nxt) * (1.0 / CMP_LEN)).astype(BF16)

    for h in range(NSA_HKV):
        q2 = jnp.concatenate([qf[:, NSA_HD * (NSA_G * h + g):NSA_HD * (NSA_G * h + g + 1)]
                              for g in range(NSA_G)], axis=0).astype(BF16)
        col = lambda slot: (slot * NSA_HKV + h) * NSA_HD

        kc = compressed(col(0))
        vc = compressed(col(1))
        s = _dot_nt(q2, kc) * NSA_SCALE
        cend = lax.broadcasted_iota(I32, (1, n_cmp), 1) * CMP_STRIDE + (CMP_LEN - 1)
        p = _softmax_rows(s, cend <= posq4)
        o_cmp = _dot(p.astype(BF16), vc)
        imp = p[0:tq]
        for g in range(1, NSA_G):
            imp = imp + p[g * tq:(g + 1) * tq]
        sel = _select_blocks(_block_importance(imp, pmat_ref[...]), posq)
        sel4 = jnp.concatenate([sel] * NSA_G, axis=0).astype(BF16)

        chosen = _dot(sel4, emat_ref[...])
        pieces = [_dot_nt(q2, pg[0, :, col(2):col(2) + NSA_HD].astype(BF16)) for pg in pages]
        pieces.append(_dot_nt(q2, new_page(kvn[:, col(2):col(2) + NSA_HD])))
        s = jnp.concatenate(pieces, axis=1) * NSA_SCALE
        n_key = s.shape[1]
        kpos = lax.broadcasted_iota(I32, (1, n_key), 1)
        p = _softmax_rows(s, (chosen > 0.5) & (kpos <= posq4)).astype(BF16)
        o_sel = _dot(p[:, past_len:], new_page(kvn[:, col(3):col(3) + NSA_HD]))
        for i, pg in enumerate(pages):
            o_sel = o_sel + _dot(p[:, i * PAGE_SIZE:(i + 1) * PAGE_SIZE],
                                 pg[0, :, col(3):col(3) + NSA_HD].astype(BF16))

        kcol, vcol = h * NSA_HD, (NSA_HKV + h) * NSA_HD
        s = jnp.concatenate([_dot_nt(q2, win_ref[0, :, kcol:kcol + NSA_HD].astype(BF16)),
                             _dot_nt(q2, new_page(kwn[:, kcol:kcol + NSA_HD]))], axis=1) * NSA_SCALE
        posw = (past_len - w_buf) + lax.broadcasted_iota(I32, (1, w_buf + PAGE_SIZE), 1)
        dpos = posq4 - posw
        p = _softmax_rows(s, (dpos >= 0) & (dpos < WINDOW) & (posw >= 0)).astype(BF16)
        o_win = (_dot(p[:, :w_buf], win_ref[0, :, vcol:vcol + NSA_HD].astype(BF16))
                 + _dot(p[:, w_buf:], new_page(kwn[:, vcol:vcol + NSA_HD])))

        for g in range(NSA_G):
            rs = slice(g * tq, (g + 1) * tq)
            gc = h * LANE + 3 * g
            head = NSA_G * h + g
            o_ref[0, :, NSA_HD * head:NSA_HD * (head + 1)] = (
                gates[:, gc:gc + 1] * o_cmp[rs] + gates[:, gc + 1:gc + 2] * o_sel[rs]
                + gates[:, gc + 2:gc + 3] * o_win[rs])

    nwin_ref[0, 0:w_buf - tq, :] = win_ref[0, tq:w_buf, :]
    nwin_ref[0, w_buf - tq:w_buf, :] = kwn


def nsa_sample(q_bf, gates, kv_new, kw_new, pool, page_table, win_buf):
    bsz, tq, _ = q_bf.shape
    n_pages = page_table.shape[1]
    past_len = n_pages * PAGE_SIZE
    w_buf = win_buf.shape[1]
    assert tq % 8 == 0 and tq <= SEL_BLOCK and past_len % SEL_BLOCK == 0
    n_cmp = past_len // CMP_STRIDE
    n_key = past_len + PAGE_SIZE
    emat = jnp.asarray(np.arange(LANE)[:, None] == (np.arange(n_key)[None, :] // SEL_BLOCK), dtype=BF16)
    seq = lambda n: pl.BlockSpec((1, tq, n), lambda b, pt: (b, 0, 0))
    page = lambda p: pl.BlockSpec((1, PAGE_SIZE, pool.shape[2]), lambda b, pt: (pt[b * n_pages + p], 0, 0))
    const = lambda shape: pl.BlockSpec(shape, lambda b, pt: (0, 0))
    return pl.pallas_call(
        functools.partial(_nsa_sample_kernel, n_pages=n_pages, past_len=past_len),
        grid_spec=pltpu.PrefetchScalarGridSpec(
            num_scalar_prefetch=1,
            grid=(bsz,),
            in_specs=[seq(q_bf.shape[2]), seq(gates.shape[2]), seq(kv_new.shape[2]), seq(kw_new.shape[2]),
                      pl.BlockSpec((1, w_buf, win_buf.shape[2]), lambda b, pt: (b, 0, 0))]
                     + [page(p) for p in range(n_pages)]
                     + [const((n_cmp, LANE)), const((LANE, n_key))],
            out_specs=[seq(q_bf.shape[2]),
                       pl.BlockSpec((1, w_buf, win_buf.shape[2]), lambda b, pt: (b, 0, 0))]),
        out_shape=[jax.ShapeDtypeStruct((bsz, tq, q_bf.shape[2]), F32),
                   jax.ShapeDtypeStruct(win_buf.shape, F32)],
        compiler_params=_params(("parallel",)),
    )(page_table.reshape(-1), q_bf, gates, kv_new, kw_new, win_buf, *([pool] * n_pages),
      _pool_matrix(n_cmp), emat)


def causal_depthwise_conv(x, buf, w):
    t = x.shape[1]
    xp = jnp.concatenate([buf, x], axis=1)
    y = sum(w[i] * xp[:, i:i + t] for i in range(w.shape[0]))
    return y, xp[:, t:]


def block_diag(x, w):
    b, t, width = x.shape
    nb, bs, _ = w.shape
    return jnp.einsum('btnc,ncd->btnd', x.reshape(b, t, nb, bs), w).reshape(b, t, width)


def rglru(xb, gb, conv_buf, h0, conv_w, conv_b, wx, bx, wa, ba, lam):
    xc, new_buf = causal_depthwise_conv(xb, conv_buf, conv_w)
    xc = xc + conv_b
    i_g = jax.nn.sigmoid(block_diag(xc, wx) + bx)
    r_g = jax.nn.sigmoid(block_diag(xc, wa) + ba)
    log_a = -LRU_C * r_g * jax.nn.softplus(-lam)
    a = jnp.exp(log_a)
    u = jnp.sqrt(-jnp.expm1(2.0 * log_a)) * (i_g * xc)
    u = u.at[:, 0].add(a[:, 0] * h0)

    def combine(l, r):
        return (l[0] * r[0], r[0] * l[1] + r[1])

    _, h = lax.associative_scan(combine, (a, u), axis=1)
    y = h * jax.nn.gelu(gb)
    return y, new_buf, h[:, -1]


def cross_attn_core(q, kv):
    b, t, _ = q.shape
    q = q.reshape(b, t, MEM_HEADS, MEM_HD)
    s = jnp.einsum('bqhd,bkhd->bhqk', q, kv[:, :, 0]) * (MEM_HD ** -0.5)
    p = jax.nn.softmax(s, axis=-1)
    return jnp.einsum('bhqk,bkhd->bqhd', p, kv[:, :, 1]).reshape(b, t, MEM_HEADS * MEM_HD)


def _gate_weight(w_in):
    per = N_GATE // NSA_HKV
    parts = [jnp.pad(w_in[:, N_MAIN0 + h * per:N_MAIN0 + (h + 1) * per], ((0, 0), (0, LANE - per)))
             for h in range(NSA_HKV)]
    return jnp.concatenate(parts, axis=1)


def kernel(x_prompt, x_sample, cache_nsa_kv, state_nsa_win_kv, state_lru_conv, state_lru_h, state_sconv, cache_mem_kv, page_table, mem_prompt, w_in0, lru_conv_w, lru_conv_b, lru_wx, lru_bx, lru_wa, lru_ba, lru_lambda, w_out0, w_in1, sconv_w, w_out1, w_q_mem, w_kv_mem, w_o_mem, w_gu, w_down, ln_g, ln_b):
    bp, tp = x_prompt.shape[:2]
    bs, ts = x_sample.shape[:2]
    assert bp == 1
    mp, ms = bp * tp, bs * ts
    past_len = page_table.shape[1] * PAGE_SIZE
    pos_all = jnp.concatenate([jnp.arange(tp, dtype=I32),
                               jnp.tile(past_len + jnp.arange(ts, dtype=I32), bs)])
    rope_cos, rope_sin = _rope_tables(pos_all)
    xa = jnp.concatenate([x_prompt.reshape(mp, D_MODEL), x_sample.reshape(ms, D_MODEL)], axis=0)
    p_nsa, p_win, p_conv, p_h, p_sc, p_mem = [], [], [], [], [], []
    s_nsa, s_win, s_conv, s_h, s_sc = [], [], [], [], []
    for l in range(DEPTH):
        if l % 2 == 0:
            e = l // 2
            lp = (lru_conv_w[e], lru_conv_b[e], lru_wx[e], lru_bx[e], lru_wa[e], lru_ba[e], lru_lambda[e])
            proj = matmul(xa, w_in0[e], n_cols=N_MAIN0)
            gt = matmul(xa, _gate_weight(w_in0[e]))
            q_bf, kv4, kv_bf, kvw, kw_bf, gates, ch = nsa_prep(proj, gt, rope_cos, rope_sin)
            xb = proj[:, :EVEN_SPLITS[0]]
            gb = proj[:, EVEN_SPLITS[0]:EVEN_SPLITS[1]]
            conv0 = jnp.zeros((bp, LRU_CONV - 1, W_LRU), F32)
            h0 = jnp.zeros((bp, W_LRU), F32)
            y_p, conv_p, h_p = rglru(xb[:mp].reshape(bp, tp, W_LRU), gb[:mp].reshape(bp, tp, W_LRU), conv0, h0, *lp)
            y_s, conv_s, h_s = rglru(xb[mp:].reshape(bs, ts, W_LRU), gb[mp:].reshape(bs, ts, W_LRU),
                                     state_lru_conv[e], state_lru_h[e], *lp)
            o_p = nsa_prompt(q_bf, gates, ch, kv_bf, kw_bf, tp)
            nq, nkv, nkw = NSA_HQ * NSA_HD, 4 * NSA_KVW, 2 * NSA_KVW
            o_s, win_s = nsa_sample(q_bf[mp:].reshape(bs, ts, nq), gates[mp:].reshape(bs, ts, -1),
                                    kv4[mp:].reshape(bs, ts, nkv), kvw[mp:].reshape(bs, ts, nkw),
                                    cache_nsa_kv[e].reshape(-1, PAGE_SIZE, nkv), page_table,
                                    state_nsa_win_kv[e].reshape(bs, -1, nkw))
            mix = jnp.concatenate([jnp.concatenate([y_p.reshape(mp, W_LRU), o_p], axis=1),
                                   jnp.concatenate([y_s.reshape(ms, W_LRU), o_s.reshape(ms, nq)], axis=1)], axis=0)
            p_nsa.append(kv4[:mp].reshape(bp, tp, 4, NSA_HKV, NSA_HD))
            p_win.append(kvw[mp - min(WINDOW, tp):mp].reshape(bp, -1, 2, NSA_HKV, NSA_HD))
            p_conv.append(conv_p)
            p_h.append(h_p)
            s_nsa.append(kv4[mp:].reshape(bs, ts, 4, NSA_HKV, NSA_HD))
            s_win.append(win_s.reshape(state_nsa_win_kv.shape[1:]))
            s_conv.append(conv_s)
            s_h.append(h_s)
            w_out = w_out0[e]
        else:
            o = l // 2
            h3 = matmul(xa, w_in1[o])
            mixes = []
            for rows, b, t, buf, acc in ((h3[:mp], bp, tp, jnp.zeros((bp, SC_CONV - 1, D_MODEL), F32), p_sc),
                                         (h3[mp:], bs, ts, state_sconv[o], s_sc)):
                bg, cg, hh = jnp.split(rows.reshape(b, t, 3 * D_MODEL), 3, axis=-1)
                y, new_buf = causal_depthwise_conv(cg * hh, buf, sconv_w[o])
                acc.append(new_buf)
                mixes.append((bg * y).reshape(b * t, D_MODEL))
            mix = jnp.concatenate(mixes, axis=0)
            w_out = w_out1[o]
        xa = matmul_postnorm(mix, w_out, xa, ln_g[l, 0], ln_b[l, 0])
        kv_mem_p = matmul(mem_prompt.reshape(bp * N_MEM, D_MODEL), w_kv_mem[l]).reshape(bp, N_MEM, 2, MEM_HEADS, MEM_HD)
        p_mem.append(kv_mem_p)
        qm = matmul(xa, w_q_mem[l])
        om = jnp.concatenate([cross_attn_core(qm[:mp].reshape(bp, tp, -1), kv_mem_p).reshape(mp, -1),
                              cross_attn_core(qm[mp:].reshape(bs, ts, -1), cache_mem_kv[l]).reshape(ms, -1)], axis=0)
        xa = matmul_postnorm(om, w_o_mem[l], xa, ln_g[l, 1], ln_b[l, 1])
        hf = matmul_swiglu(xa, w_gu[l])
        xa = matmul_postnorm(hf, w_down[l], xa, ln_g[l, 2], ln_b[l, 2])
    return (xa[:mp].reshape(bp, tp, D_MODEL), xa[mp:].reshape(bs, ts, D_MODEL),
            jnp.stack(p_nsa), jnp.stack(p_win), jnp.stack(p_conv), jnp.stack(p_h), jnp.stack(p_sc),
            jnp.stack(p_mem), jnp.stack(s_nsa), jnp.stack(s_win), jnp.stack(s_conv), jnp.stack(s_h),
            jnp.stack(s_sc))
```

```python
import functools

import numpy as np
import jax
import jax.numpy as jnp
from jax import lax
from jax.experimental import pallas as pl
from jax.experimental.pallas import tpu as pltpu

D_MODEL = 2048
SEQ = 8192
DEPTH = 2
DEC_BATCH = 128
DEC_SEQ = 8
PAGE_SIZE = 128

ALPHA = (2.0 * DEPTH) ** 0.25
LN_EPS = 1e-5
W_LRU = D_MODEL // 2
LRU_CONV = 4
LRU_BLOCKS = 8
LRU_BLOCK = W_LRU // LRU_BLOCKS
LRU_C = 8.0
NSA_HQ = 8
NSA_HKV = 2
NSA_HD = 128
NSA_G = NSA_HQ // NSA_HKV
NSA_KVW = NSA_HKV * NSA_HD
CMP_STRIDE = 16
CMP_LEN = 2 * CMP_STRIDE
SEL_BLOCK = 64
SEL_TOPK = 16
WINDOW = 512
Q_BLOCK = 128
ROPE_THETA = 10000.0
BIG = 1e6
NEG = -1e30
NSA_SCALE = NSA_HD ** -0.5
EVEN_SPLITS = [W_LRU, 2 * W_LRU, 2 * W_LRU + NSA_HQ * NSA_HD,
               2 * W_LRU + NSA_HQ * NSA_HD + 4 * NSA_KVW,
               2 * W_LRU + NSA_HQ * NSA_HD + 6 * NSA_KVW]
N_MAIN0 = EVEN_SPLITS[-1]
N_GATE = 3 * NSA_HQ
SC_CONV = 3
N_MEM = 256
MEM_HEADS = 4
MEM_HD = D_MODEL // MEM_HEADS
D_FF = ((8 * D_MODEL + 3 * 256 - 1) // (3 * 256)) * 256

LANE = 128
VMEM_LIMIT = 56 * 1024 * 1024
SEL_TILE = 512

F32 = jnp.float32
BF16 = jnp.bfloat16
I32 = jnp.int32


def _params(sem):
    return pltpu.CompilerParams(dimension_semantics=sem, vmem_limit_bytes=VMEM_LIMIT)


def _dot(a, b):
    return jnp.dot(a, b, preferred_element_type=F32)


def _dot_nt(a, b):
    return lax.dot_general(a, b, (((1,), (1,)), ((), ())), preferred_element_type=F32)


def _mm_kernel(x_ref, w_ref, o_ref, *, nk):
    part = _dot(x_ref[...].astype(BF16), w_ref[...].astype(BF16))
    if nk == 1:
        o_ref[...] = part
    else:
        k = pl.program_id(2)

        @pl.when(k == 0)
        def _():
            o_ref[...] = part

        @pl.when(k > 0)
        def _():
            o_ref[...] += part


def matmul(x, w, *, n_cols=None, tm=1024, tn=512, tk=None):
    m, kdim = x.shape
    n = w.shape[1] if n_cols is None else n_cols
    tm = min(tm, m)
    tn = min(tn, n)
    tk = kdim if tk is None else tk
    assert m % tm == 0 and n % tn == 0 and kdim % tk == 0
    nk = kdim // tk
    return pl.pallas_call(
        functools.partial(_mm_kernel, nk=nk),
        grid=(m // tm, n // tn, nk),
        in_specs=[pl.BlockSpec((tm, tk), lambda i, j, k: (i, k)),
                  pl.BlockSpec((tk, tn), lambda i, j, k: (k, j))],
        out_specs=pl.BlockSpec((tm, tn), lambda i, j, k: (i, j)),
        out_shape=jax.ShapeDtypeStruct((m, n), F32),
        compiler_params=_params(("parallel", "parallel", "arbitrary")),
        name="matmul",
    )(x, w)


def _mm_ln_kernel(x_ref, w_ref, res_ref, g_ref, b_ref, o_ref, acc_ref, *, nk):
    k = pl.program_id(1)
    part = _dot(x_ref[...].astype(BF16), w_ref[...].astype(BF16))

    @pl.when(k == 0)
    def _():
        acc_ref[...] = part

    @pl.when(k > 0)
    def _():
        acc_ref[...] += part

    @pl.when(k == nk - 1)
    def _():
        z = ALPHA * res_ref[...] + acc_ref[...]
        mu = jnp.mean(z, axis=-1, keepdims=True)
        zc = z - mu
        var = jnp.mean(zc * zc, axis=-1, keepdims=True)
        o_ref[...] = zc * lax.rsqrt(var + LN_EPS) * g_ref[...] + b_ref[...]


def matmul_postnorm(x, w, res, g, b, *, tm=512, tk=512):
    m, kdim = x.shape
    n = w.shape[1]
    assert m % tm == 0 and kdim % tk == 0 and res.shape == (m, n)
    nk = kdim // tk
    return pl.pallas_call(
        functools.partial(_mm_ln_kernel, nk=nk),
        grid=(m // tm, nk),
        in_specs=[pl.BlockSpec((tm, tk), lambda i, k: (i, k)),
                  pl.BlockSpec((tk, n), lambda i, k: (k, 0)),
                  pl.BlockSpec((tm, n), lambda i, k: (i, 0)),
                  pl.BlockSpec((1, n), lambda i, k: (0, 0)),
                  pl.BlockSpec((1, n), lambda i, k: (0, 0))],
        out_specs=pl.BlockSpec((tm, n), lambda i, k: (i, 0)),
        out_shape=jax.ShapeDtypeStruct((m, n), F32),
        scratch_shapes=[pltpu.VMEM((tm, n), F32)],
        compiler_params=_params(("parallel", "arbitrary")),
        name="matmul_postnorm",
    )(x, w, res, g.reshape(1, n), b.reshape(1, n))


def _mm_swiglu_kernel(x_ref, wg_ref, wu_ref, o_ref):
    xb = x_ref[...].astype(BF16)
    g = _dot(xb, wg_ref[...].astype(BF16))
    u = _dot(xb, wu_ref[...].astype(BF16))
    o_ref[...] = g * jax.nn.sigmoid(g) * u


def matmul_swiglu(x, w_gu, *, tm=512, tn=512):
    m, kdim = x.shape
    f = w_gu.shape[1] // 2
    assert m % tm == 0 and f % tn == 0
    nj = f // tn
    return pl.pallas_call(
        _mm_swiglu_kernel,
        grid=(m // tm, nj),
        in_specs=[pl.BlockSpec((tm, kdim), lambda i, j: (i, 0)),
                  pl.BlockSpec((kdim, tn), lambda i, j: (0, j)),
                  pl.BlockSpec((kdim, tn), lambda i, j: (0, j + nj))],
        out_specs=pl.BlockSpec((tm, tn), lambda i, j: (i, j)),
        out_shape=jax.ShapeDtypeStruct((m, f), F32),
        compiler_params=_params(("parallel", "parallel")),
        name="matmul_swiglu",
    )(x, w_gu, w_gu)


def _rope_tables(pos):
    half = NSA_HD // 2
    inv = ROPE_THETA ** (-jnp.arange(half, dtype=F32) / half)
    ang = pos.astype(F32)[:, None] * inv[None, :]
    cos, sin = jnp.cos(ang), jnp.sin(ang)
    return jnp.concatenate([cos, cos], axis=-1), jnp.concatenate([-sin, sin], axis=-1)


def _nsa_prep_kernel(q_ref, kv_ref, kw_ref, gt_ref, cos_ref, sin_ref,
                     qbf_ref, kv_o_ref, kvbf_ref, kw_o_ref, kwbf_ref, g_o_ref, ch_ref):
    cos = cos_ref[...]
    sin = sin_ref[...]
    rows = q_ref.shape[0]

    def rot(x):
        return x * cos + pltpu.roll(x, NSA_HD // 2, axis=1) * sin

    for h in range(NSA_HQ):
        sl = slice(NSA_HD * h, NSA_HD * (h + 1))
        qbf_ref[:, sl] = rot(q_ref[:, sl]).astype(BF16)
    for c in range(4 * NSA_HKV):
        sl = slice(NSA_HD * c, NSA_HD * (c + 1))
        slot = c // NSA_HKV
        x = kv_ref[:, sl]
        if slot in (0, 2):
            x = rot(x)
        kv_o_ref[:, sl] = x
        kvbf_ref[:, sl] = x.astype(BF16)
        if slot < 2:
            ch_ref[:, sl] = jnp.sum(x.reshape(rows // CMP_STRIDE, CMP_STRIDE, NSA_HD), axis=1)
    for c in range(2 * NSA_HKV):
        sl = slice(NSA_HD * c, NSA_HD * (c + 1))
        x = kw_ref[:, sl]
        if c // NSA_HKV == 0:
            x = rot(x)
        kw_o_ref[:, sl] = x
        kwbf_ref[:, sl] = x.astype(BF16)
    g_o_ref[...] = jax.nn.sigmoid(gt_ref[...])


def nsa_prep(proj, gt, cos, sin, *, tm=512):
    m = proj.shape[0]
    assert m % tm == 0
    qc = EVEN_SPLITS[1] // (NSA_HQ * NSA_HD)
    kc = EVEN_SPLITS[2] // (4 * NSA_KVW)
    wc = EVEN_SPLITS[3] // (2 * NSA_KVW)
    nq, nkv, nkw, ng = NSA_HQ * NSA_HD, 4 * NSA_KVW, 2 * NSA_KVW, NSA_HKV * LANE
    row = lambda n: pl.BlockSpec((tm, n), lambda i: (i, 0))
    return pl.pallas_call(
        _nsa_prep_kernel,
        grid=(m // tm,),
        in_specs=[pl.BlockSpec((tm, nq), lambda i: (i, qc)),
                  pl.BlockSpec((tm, nkv), lambda i: (i, kc)),
                  pl.BlockSpec((tm, nkw), lambda i: (i, wc)),
                  row(ng), row(NSA_HD), row(NSA_HD)],
        out_specs=[row(nq), row(nkv), row(nkv), row(nkw), row(nkw), row(ng),
                   pl.BlockSpec((tm // CMP_STRIDE, 2 * NSA_KVW), lambda i: (i, 0))],
        out_shape=[jax.ShapeDtypeStruct((m, nq), BF16),
                   jax.ShapeDtypeStruct((m, nkv), F32),
                   jax.ShapeDtypeStruct((m, nkv), BF16),
                   jax.ShapeDtypeStruct((m, nkw), F32),
                   jax.ShapeDtypeStruct((m, nkw), BF16),
                   jax.ShapeDtypeStruct((m, ng), F32),
                   jax.ShapeDtypeStruct((m // CMP_STRIDE, 2 * NSA_KVW), F32)],
        compiler_params=_params(("parallel",)),
        name="nsa_prep",
    )(proj, proj, proj, gt, cos, sin)


def _softmax_rows(s, mask):
    sm = jnp.where(mask, s, NEG)
    e = jnp.where(mask, jnp.exp(sm - jnp.max(sm, axis=-1, keepdims=True)), 0.0)
    return e / jnp.maximum(jnp.sum(e, axis=-1, keepdims=True), 1e-30)


def _block_importance(imp, pmat):
    hi = imp.astype(BF16)
    r1 = imp - hi.astype(F32)
    mid = r1.astype(BF16)
    lo = (r1 - mid.astype(F32)).astype(BF16)
    return _dot(hi, pmat) + _dot(mid, pmat) + _dot(lo, pmat)


def _select_blocks(imp_sel, posq):
    blk = lax.broadcasted_iota(I32, (1, LANE), 1)
    cur = posq // SEL_BLOCK
    forced = (blk == 0) | (blk == cur) | (blk == cur - 1)
    valid = blk * SEL_BLOCK <= posq
    score = jnp.where(valid, jnp.where(forced, BIG, imp_sel), -BIG)
    key = lax.bitcast_convert_type(score, I32)
    key = jnp.where(key < 0, key ^ 0x7FFFFFFF, key)
    key_m1 = key - 1
    cnt = jnp.zeros(score.shape, I32)
    for i in range(LANE):
        thr = jnp.where(blk > i, key_m1, key)
        cnt = cnt + (key[:, i:i + 1] > thr).astype(I32)
    return (cnt < SEL_TOPK).astype(F32)


def _pool_matrix(n_cmp):
    c = np.arange(n_cmp)[:, None]
    j = np.arange(LANE)[None, :]
    return jnp.asarray(((c // 4 == j).astype(np.float32) + ((c + 1) // 4 == j)), dtype=BF16)


def _nsa_prompt_kernel(q_ref, g_ref, chk_ref, chv_ref, ks_ref, vs_ref, kw_ref, vw_ref, pmat_ref,
                       o_ref, kct_ref, vc_ref):
    qb = pl.program_id(1)
    n_cmp = chk_ref.shape[0]

    @pl.when(qb == 0)
    def _():
        ck = chk_ref[...]
        kc = (ck + pltpu.roll(ck, n_cmp - 1, axis=0)) * (1.0 / CMP_LEN)
        kct_ref[...] = kc.T.astype(BF16)
        cv = chv_ref[...]
        vc_ref[...] = ((cv + pltpu.roll(cv, n_cmp - 1, axis=0)) * (1.0 / CMP_LEN)).astype(BF16)

    qblk = q_ref[...]
    q2 = jnp.concatenate([qblk[:, NSA_HD * g:NSA_HD * (g + 1)] for g in range(NSA_G)], axis=0)
    rows = NSA_G * Q_BLOCK
    posq = qb * Q_BLOCK + lax.broadcasted_iota(I32, (Q_BLOCK, 1), 0)
    posq4 = jnp.concatenate([posq] * NSA_G, axis=0)

    s = _dot(q2, kct_ref[...]) * NSA_SCALE
    cend = lax.broadcasted_iota(I32, (1, n_cmp), 1) * CMP_STRIDE + (CMP_LEN - 1)
    p = _softmax_rows(s, cend <= posq4)
    o_cmp = _dot(p.astype(BF16), vc_ref[...])
    imp = p[0:Q_BLOCK]
    for g in range(1, NSA_G):
        imp = imp + p[g * Q_BLOCK:(g + 1) * Q_BLOCK]
    sel = _select_blocks(_block_importance(imp, pmat_ref[...]), posq).astype(BF16)

    brow = lax.broadcasted_iota(I32, (LANE, 1), 0)
    kcol = lax.broadcasted_iota(I32, (1, SEL_TILE), 1)
    kdiv = kcol // SEL_BLOCK

    def body(t, carry):
        m, l, acc = carry
        off = pl.multiple_of(t * SEL_TILE, SEL_TILE)
        k = ks_ref[pl.ds(off, SEL_TILE), :]
        v = vs_ref[pl.ds(off, SEL_TILE), :]
        st = _dot_nt(q2, k) * NSA_SCALE
        expand = (brow == kdiv + t * (SEL_TILE // SEL_BLOCK)).astype(BF16)
        chosen = _dot(sel, expand)
        msk = (chosen > 0.5) & (kcol + t * SEL_TILE <= posq)
        sm = st.reshape(NSA_G, Q_BLOCK, SEL_TILE) + jnp.where(msk, 0.0, NEG)[None]
        m_new = jnp.maximum(m, jnp.max(sm, axis=-1, keepdims=True))
        alpha = jnp.exp(m - m_new)
        e = jnp.exp(sm - m_new)
        l = alpha * l + jnp.sum(e, axis=-1, keepdims=True)
        pv = _dot(e.reshape(rows, SEL_TILE).astype(BF16), v)
        acc = alpha.reshape(rows, 1) * acc + pv
        return m_new, l, acc

    m0 = jnp.full((NSA_G, Q_BLOCK, 1), NEG, F32)
    l0 = jnp.zeros((NSA_G, Q_BLOCK, 1), F32)
    n_tiles = qb // (SEL_TILE // Q_BLOCK) + 1
    _, l, acc = lax.fori_loop(0, n_tiles, body, (m0, l0, jnp.zeros((rows, NSA_HD), F32)))
    o_sel = acc / jnp.maximum(l.reshape(rows, 1), 1e-30)

    span = WINDOW + Q_BLOCK
    start = pl.multiple_of(jnp.maximum(qb - WINDOW // Q_BLOCK, 0) * Q_BLOCK, Q_BLOCK)
    kw = kw_ref[pl.ds(start, span), :]
    vw = vw_ref[pl.ds(start, span), :]
    s = _dot_nt(q2, kw) * NSA_SCALE
    dpos = posq4 - (start + lax.broadcasted_iota(I32, (1, span), 1))
    p = _softmax_rows(s, (dpos >= 0) & (dpos < WINDOW))
    o_win = _dot(p.astype(BF16), vw)

    gates = g_ref[...]
    for g in range(NSA_G):
        rs = slice(g * Q_BLOCK, (g + 1) * Q_BLOCK)
        o_ref[:, NSA_HD * g:NSA_HD * (g + 1)] = (gates[:, 3 * g:3 * g + 1] * o_cmp[rs]
                                                + gates[:, 3 * g + 1:3 * g + 2] * o_sel[rs]
                                                + gates[:, 3 * g + 2:3 * g + 3] * o_win[rs])


def nsa_prompt(q_bf, gates, ch, kv_bf, kw_bf, t):
    n_qb = t // Q_BLOCK
    n_cmp = t // CMP_STRIDE
    gw = NSA_G * NSA_HD
    res = lambda c0: pl.BlockSpec((t, NSA_HD), lambda h, i: (0, c0 + h))
    return pl.pallas_call(
        _nsa_prompt_kernel,
        grid=(NSA_HKV, n_qb),
        in_specs=[pl.BlockSpec((Q_BLOCK, gw), lambda h, i: (i, h)),
                  pl.BlockSpec((Q_BLOCK, LANE), lambda h, i: (i, h)),
                  pl.BlockSpec((n_cmp, NSA_HD), lambda h, i: (0, h)),
                  pl.BlockSpec((n_cmp, NSA_HD), lambda h, i: (0, NSA_HKV + h)),
                  res(2 * NSA_HKV), res(3 * NSA_HKV), res(0), res(NSA_HKV),
                  pl.BlockSpec((n_cmp, LANE), lambda h, i: (0, 0))],
        out_specs=pl.BlockSpec((Q_BLOCK, gw), lambda h, i: (i, h)),
        out_shape=jax.ShapeDtypeStruct((q_bf.shape[0], NSA_HQ * NSA_HD), F32),
        scratch_shapes=[pltpu.VMEM((NSA_HD, n_cmp), BF16), pltpu.VMEM((n_cmp, NSA_HD), BF16)],
        compiler_params=_params(("parallel", "arbitrary")),
        name="nsa_prompt",
    )(q_bf, gates, ch, ch, kv_bf, kv_bf, kw_bf, kw_bf, _pool_matrix(n_cmp))


def _nsa_sample_kernel(pt_ref, q_ref, g_ref, kvn_ref, kwn_ref, win_ref, *rest, n_pages, past_len):
    pages = rest[:n_pages]
    pmat_ref, emat_ref, _, o_ref, nwin_ref = rest[n_pages:]
    tq = q_ref.shape[1]
    w_buf = win_ref.shape[1]
    pad_rows = PAGE_SIZE - tq
    qf = q_ref[0].astype(F32)
    kvn = kvn_ref[...]
    kwn = kwn_ref[...]
    gates = g_ref[...]
    posq = past_len + lax.broadcasted_iota(I32, (tq, 1), 0)
    posq4 = jnp.concatenate([posq] * NSA_G, axis=0)
    zpad = jnp.zeros((pad_rows, NSA_HD), F32)

    def new_page(x):
        return jnp.concatenate([x, zpad], axis=0).astype(BF16)

    def chunk_sums(x):
        return jnp.sum(x.reshape(x.shape[0] // CMP_STRIDE, CMP_STRIDE, NSA_HD), axis=1)

    n_cmp = past_len // CMP_STRIDE
    crow = lax.broadcasted_iota(I32, (n_cmp, 1), 0)

    def compressed(col):
        ch = jnp.concatenate([chunk_sums(pg[0, :, col:col + NSA_HD]) for pg in pages], axis=0)
        ch_new = jnp.sum(kvn[:, col:col + NSA_HD], axis=0, keepdims=True)
        nxt = jnp.where(crow == n_cmp - 1, ch_new, pltpu.roll(ch, n_cmp - 1, axis=0))
        return ((ch + nxt) * (1.0 / CMP_LEN)).astype(BF16)

    for h in range(NSA_HKV):
        q2 = jnp.concatenate([qf[:, NSA_HD * (NSA_G * h + g):NSA_HD * (NSA_G * h + g + 1)]
                              for g in range(NSA_G)], axis=0).astype(BF16)
        col = lambda slot: (slot * NSA_HKV + h) * NSA_HD

        kc = compressed(col(0))
        vc = compressed(col(1))
        s = _dot_nt(q2, kc) * NSA_SCALE
        cend = lax.broadcasted_iota(I32, (1, n_cmp), 1) * CMP_STRIDE + (CMP_LEN - 1)
        p = _softmax_rows(s, cend <= posq4)
        o_cmp = _dot(p.astype(BF16), vc)
        imp = p[0:tq]
        for g in range(1, NSA_G):
            imp = imp + p[g * tq:(g + 1) * tq]
        sel = _select_blocks(_block_importance(imp, pmat_ref[...]), posq)
        sel4 = jnp.concatenate([sel] * NSA_G, axis=0).astype(BF16)

        chosen = _dot(sel4, emat_ref[...])
        pieces = [_dot_nt(q2, pg[0, :, col(2):col(2) + NSA_HD].astype(BF16)) for pg in pages]
        pieces.append(_dot_nt(q2, new_page(kvn[:, col(2):col(2) + NSA_HD])))
        s = jnp.concatenate(pieces, axis=1) * NSA_SCALE
        n_key = s.shape[1]
        kpos = lax.broadcasted_iota(I32, (1, n_key), 1)
        p = _softmax_rows(s, (chosen > 0.5) & (kpos <= posq4)).astype(BF16)
        o_sel = _dot(p[:, past_len:], new_page(kvn[:, col(3):col(3) + NSA_HD]))
        for i, pg in enumerate(pages):
            o_sel = o_sel + _dot(p[:, i * PAGE_SIZE:(i + 1) * PAGE_SIZE],
                                 pg[0, :, col(3):col(3) + NSA_HD].astype(BF16))

        kcol, vcol = h * NSA_HD, (NSA_HKV + h) * NSA_HD
        s = jnp.concatenate([_dot_nt(q2, win_ref[0, :, kcol:kcol + NSA_HD].astype(BF16)),
                             _dot_nt(q2, new_page(kwn[:, kcol:kcol + NSA_HD]))], axis=1) * NSA_SCALE
        posw = (past_len - w_buf) + lax.broadcasted_iota(I32, (1, w_buf + PAGE_SIZE), 1)
        dpos = posq4 - posw
        p = _softmax_rows(s, (dpos >= 0) & (dpos < WINDOW) & (posw >= 0)).astype(BF16)
        o_win = (_dot(p[:, :w_buf], win_ref[0, :, vcol:vcol + NSA_HD].astype(BF16))
                 + _dot(p[:, w_buf:], new_page(kwn[:, vcol:vcol + NSA_HD])))

        for g in range(NSA_G):
            rs = slice(g * tq, (g + 1) * tq)
            gc = h * LANE + 3 * g
            head = NSA_G * h + g
            o_ref[:, NSA_HD * head:NSA_HD * (head + 1)] = (
                gates[:, gc:gc + 1] * o_cmp[rs] + gates[:, gc + 1:gc + 2] * o_sel[rs]
                + gates[:, gc + 2:gc + 3] * o_win[rs])

    nwin_ref[0, 0:w_buf - tq, :] = win_ref[0, tq:w_buf, :]
    nwin_ref[0, w_buf - tq:w_buf, :] = kwn


def nsa_sample(q_bf, gates, kv_new, kw_new, mp, pool, page_base, page_table, win_all, win_base, o_buf):
    bsz, tq, nq = q_bf.shape
    n_pages = page_table.shape[1]
    past_len = n_pages * PAGE_SIZE
    w_buf, nkw = win_all.shape[1:]
    assert tq == 8 and past_len % SEL_BLOCK == 0 and mp % tq == 0
    n_cmp = past_len // CMP_STRIDE
    n_key = past_len + PAGE_SIZE
    rb = mp // tq
    emat = jnp.asarray(np.arange(LANE)[:, None] == (np.arange(n_key)[None, :] // SEL_BLOCK), dtype=BF16)
    seq = lambda n: pl.BlockSpec((tq, n), lambda b, pt: (rb + b, 0))
    page = lambda p: pl.BlockSpec((1, PAGE_SIZE, pool.shape[2]),
                                  lambda b, pt: (page_base + pt[b * n_pages + p], 0, 0))
    const = lambda shape: pl.BlockSpec(shape, lambda b, pt: (0, 0))
    return pl.pallas_call(
        functools.partial(_nsa_sample_kernel, n_pages=n_pages, past_len=past_len),
        grid_spec=pltpu.PrefetchScalarGridSpec(
            num_scalar_prefetch=1,
            grid=(bsz,),
            in_specs=[pl.BlockSpec((1, tq, nq), lambda b, pt: (b, 0, 0)),
                      seq(gates.shape[1]), seq(kv_new.shape[1]), seq(kw_new.shape[1]),
                      pl.BlockSpec((1, w_buf, nkw), lambda b, pt: (win_base + b, 0, 0))]
                     + [page(p) for p in range(n_pages)]
                     + [const((n_cmp, LANE)), const((LANE, n_key)), pl.BlockSpec(memory_space=pl.ANY)],
            out_specs=[seq(nq), pl.BlockSpec((1, w_buf, nkw), lambda b, pt: (b, 0, 0))]),
        out_shape=[jax.ShapeDtypeStruct(o_buf.shape, F32),
                   jax.ShapeDtypeStruct((bsz, w_buf, nkw), F32)],
        input_output_aliases={8 + n_pages: 0},
        compiler_params=_params(("parallel",)),
        name="nsa_sample",
    )(page_table.reshape(-1), q_bf, gates, kv_new, kw_new, win_all, *([pool] * n_pages),
      _pool_matrix(n_cmp), emat, o_buf)


def _shift_rows(x, s):
    return pltpu.roll(x, s % x.shape[0], axis=0)


def _lru_recurrence_inputs(xc, wx_ref, bx_ref, wa_ref, ba_ref, lam_ref):
    xb = xc.astype(BF16)
    i_g = jax.nn.sigmoid(_dot(xb, wx_ref[0].astype(BF16)) + bx_ref[...])
    r_g = jax.nn.sigmoid(_dot(xb, wa_ref[0].astype(BF16)) + ba_ref[...])
    z = -lam_ref[...]
    softplus = jnp.maximum(z, 0.0) + jnp.log1p(jnp.exp(-jnp.abs(z)))
    log_a = -LRU_C * r_g * softplus
    a = jnp.exp(log_a)
    th = jnp.tanh(log_a)
    u = jnp.sqrt(-2.0 * th / (1.0 - th)) * (i_g * xc)
    return a, u


def _scan_rows(a, u, tpos, length):
    d = 1
    while d < length:
        keep = tpos >= d
        a_prev = jnp.where(keep, _shift_rows(a, d), 1.0)
        u_prev = jnp.where(keep, _shift_rows(u, d), 0.0)
        u = a * u_prev + u
        a = a * a_prev
        d *= 2
    return a, u


def _lru_prompt_kernel(x_ref, g_ref, cw_ref, cb_ref, wx_ref, bx_ref, wa_ref, ba_ref, lam_ref,
                       y_ref, hl_ref, tail_ref, hs_ref):
    tt = pl.program_id(1)
    rows = x_ref.shape[0]

    @pl.when(tt == 0)
    def _():
        tail_ref[...] = jnp.zeros(tail_ref.shape, F32)
        hs_ref[...] = jnp.zeros(hs_ref.shape, F32)

    x = x_ref[...]
    w = cw_ref[...]
    tail = tail_ref[...]
    row8 = lax.broadcasted_iota(I32, (8, LRU_BLOCK), 0)
    acc = None
    for i in range(LRU_CONV):
        s = LRU_CONV - 1 - i
        if s == 0:
            xs = x
        else:
            xs = _shift_rows(x, s)
            head = jnp.where(row8 < s, _shift_rows(tail, s), xs[0:8])
            xs = jnp.concatenate([head, xs[8:]], axis=0)
        term = w[i:i + 1] * xs
        acc = term if acc is None else acc + term
    xc = acc + cb_ref[...]
    tail_ref[...] = x[rows - 8:rows]

    a, u = _lru_recurrence_inputs(xc, wx_ref, bx_ref, wa_ref, ba_ref, lam_ref)
    tpos = lax.broadcasted_iota(I32, (rows, 1), 0)
    a, u = _scan_rows(a, u, tpos, rows)
    h = u + a * hs_ref[7:8, :]
    hs_ref[...] = h[rows - 8:rows]
    hl_ref[...] = h[rows - 1:rows]
    y_ref[...] = h * jax.nn.gelu(g_ref[...])


def _lru_sample_kernel(x_ref, g_ref, buf_ref, h0_ref, cw_ref, cb_ref, wx_ref, bx_ref, wa_ref, ba_ref,
                       lam_ref, mix_ref, y_ref, h_ref, *, tq):
    del mix_ref
    rows = x_ref.shape[0]
    x = x_ref[...]
    w = cw_ref[...]
    buf = buf_ref[...]
    tpos = lax.broadcasted_iota(I32, (rows, 1), 0) % tq
    acc = None
    for i in range(LRU_CONV):
        s = LRU_CONV - 1 - i
        xs = x if s == 0 else jnp.where(tpos >= s, _shift_rows(x, s), _shift_rows(buf, -i))
        term = w[i:i + 1] * xs
        acc = term if acc is None else acc + term
    xc = acc + cb_ref[...]
    a, u = _lru_recurrence_inputs(xc, wx_ref, bx_ref, wa_ref, ba_ref, lam_ref)
    a, u = _scan_rows(a, u, tpos, tq)
    h = u + a * h0_ref[...]
    h_ref[...] = h
    y_ref[...] = h * jax.nn.gelu(g_ref[...])


def rglru(proj, mp, conv_state, h0, conv_w, conv_b, wx, bx, wa, ba, lam, *, bs, ts, tt=512):
    m = proj.shape[0]
    ms = bs * ts
    nb = LRU_BLOCKS
    assert mp % tt == 0 and mp % ms == 0 and ts == 8
    vec = lambda v: v.reshape(1, W_LRU)
    cspec = lambda shape, im: pl.BlockSpec(shape, im)
    par_p = [cspec((LRU_CONV, LRU_BLOCK), lambda c, t: (0, c)), cspec((1, LRU_BLOCK), lambda c, t: (0, c)),
             cspec((1, LRU_BLOCK, LRU_BLOCK), lambda c, t: (c, 0, 0)), cspec((1, LRU_BLOCK), lambda c, t: (0, c)),
             cspec((1, LRU_BLOCK, LRU_BLOCK), lambda c, t: (c, 0, 0)), cspec((1, LRU_BLOCK), lambda c, t: (0, c)),
             cspec((1, LRU_BLOCK), lambda c, t: (0, c))]
    params = (conv_w, vec(conv_b), wx, vec(bx), wa, vec(ba), vec(lam))
    y, h_last = pl.pallas_call(
        _lru_prompt_kernel,
        grid=(nb, mp // tt),
        in_specs=[pl.BlockSpec((tt, LRU_BLOCK), lambda c, t: (t, c)),
                  pl.BlockSpec((tt, LRU_BLOCK), lambda c, t: (t, nb + c))] + par_p,
        out_specs=[pl.BlockSpec((tt, LRU_BLOCK), lambda c, t: (t, c)),
                   pl.BlockSpec((1, LRU_BLOCK), lambda c, t: (0, c))],
        out_shape=[jax.ShapeDtypeStruct((m, W_LRU), F32), jax.ShapeDtypeStruct((1, W_LRU), F32)],
        scratch_shapes=[pltpu.VMEM((8, LRU_BLOCK), F32), pltpu.VMEM((8, LRU_BLOCK), F32)],
        compiler_params=_params(("parallel", "arbitrary")),
        name="lru_prompt",
    )(proj, proj, *params)

    buf = jnp.pad(conv_state, ((0, 0), (0, ts - (LRU_CONV - 1)), (0, 0))).reshape(ms, W_LRU)
    h0r = jnp.repeat(h0, ts, axis=0)
    rb = mp // ms
    par_s = [cspec((LRU_CONV, LRU_BLOCK), lambda c: (0, c)), cspec((1, LRU_BLOCK), lambda c: (0, c)),
             cspec((1, LRU_BLOCK, LRU_BLOCK), lambda c: (c, 0, 0)), cspec((1, LRU_BLOCK), lambda c: (0, c)),
             cspec((1, LRU_BLOCK, LRU_BLOCK), lambda c: (c, 0, 0)), cspec((1, LRU_BLOCK), lambda c: (0, c)),
             cspec((1, LRU_BLOCK), lambda c: (0, c))]
    y, h_s = pl.pallas_call(
        functools.partial(_lru_sample_kernel, tq=ts),
        grid=(nb,),
        in_specs=[pl.BlockSpec((ms, LRU_BLOCK), lambda c: (rb, c)),
                  pl.BlockSpec((ms, LRU_BLOCK), lambda c: (rb, nb + c)),
                  pl.BlockSpec((ms, LRU_BLOCK), lambda c: (0, c)),
                  pl.BlockSpec((ms, LRU_BLOCK), lambda c: (0, c))] + par_s
                 + [pl.BlockSpec(memory_space=pl.ANY)],
        out_specs=[pl.BlockSpec((ms, LRU_BLOCK), lambda c: (rb, c)),
                   pl.BlockSpec((ms, LRU_BLOCK), lambda c: (0, c))],
        out_shape=[jax.ShapeDtypeStruct((m, W_LRU), F32), jax.ShapeDtypeStruct((ms, W_LRU), F32)],
        input_output_aliases={4 + len(par_s): 0},
        compiler_params=_params(("parallel",)),
        name="lru_sample",
    )(proj, proj, buf, h0r, *params, y)
    return y, h_last, h_s


def _sconv_kernel(h_ref, prev_ref, buf_ref, w_ref, o_ref, z_ref, *, n_prompt_tiles, tq):
    i = pl.program_id(0)
    rows = h_ref.shape[0]
    n = o_ref.shape[1]
    bg = h_ref[:, 0:n]
    z = h_ref[:, n:2 * n] * h_ref[:, 2 * n:3 * n]
    z_ref[...] = z
    w = w_ref[...]

    def finish(fix):
        acc = None
        for k in range(SC_CONV):
            s = SC_CONV - 1 - k
            zs = z if s == 0 else fix(_shift_rows(z, s), s, k)
            term = w[k:k + 1] * zs
            acc = term if acc is None else acc + term
        o_ref[...] = bg * acc

    @pl.when(i < n_prompt_tiles)
    def _():
        pz = prev_ref[:, n:2 * n] * prev_ref[:, 2 * n:3 * n]
        pz = jnp.where(i > 0, pz, 0.0)
        row8 = lax.broadcasted_iota(I32, (8, 1), 0)

        def fix(zs, s, k):
            head = jnp.where(row8 < s, _shift_rows(pz, s), zs[0:8])
            return jnp.concatenate([head, zs[8:]], axis=0)

        finish(fix)

    @pl.when(i >= n_prompt_tiles)
    def _():
        tpos = lax.broadcasted_iota(I32, (rows, 1), 0) % tq
        buf = buf_ref[...]
        finish(lambda zs, s, k: jnp.where(tpos >= s, zs, _shift_rows(buf, -k)))


def short_conv(h3, mp, conv_state, conv_w, *, ts, tm=256):
    m = h3.shape[0]
    n = h3.shape[1] // 3
    ms = m - mp
    assert mp % tm == 0 and ms % tm == 0 and ts == 8
    npt = mp // tm
    buf = jnp.pad(conv_state, ((0, 0), (0, ts - (SC_CONV - 1)), (0, 0))).reshape(ms, n)
    return pl.pallas_call(
        functools.partial(_sconv_kernel, n_prompt_tiles=npt, tq=ts),
        grid=(m // tm,),
        in_specs=[pl.BlockSpec((tm, 3 * n), lambda i: (i, 0)),
                  pl.BlockSpec((8, 3 * n), lambda i: (jnp.maximum(jnp.minimum(i, npt) * (tm // 8) - 1, 0), 0)),
                  pl.BlockSpec((tm, n), lambda i: (jnp.maximum(i - npt, 0), 0)),
                  pl.BlockSpec((SC_CONV, n), lambda i: (0, 0))],
        out_specs=[pl.BlockSpec((tm, n), lambda i: (i, 0)), pl.BlockSpec((tm, n), lambda i: (i, 0))],
        out_shape=[jax.ShapeDtypeStruct((m, n), F32), jax.ShapeDtypeStruct((m, n), F32)],
        compiler_params=_params(("parallel",)),
        name="short_conv",
    )(h3, h3, buf, conv_w)


def _softmax_plain(s):
    e = jnp.exp(s - jnp.max(s, axis=-1, keepdims=True))
    return e / jnp.sum(e, axis=-1, keepdims=True)


def _xattn_heads(q, kv, o_ref):
    width = MEM_HEADS * MEM_HD
    for h in range(MEM_HEADS):
        sl = slice(h * MEM_HD, (h + 1) * MEM_HD)
        kh = kv[:, sl].astype(BF16)
        vh = kv[:, width + h * MEM_HD:width + (h + 1) * MEM_HD].astype(BF16)
        p = _softmax_plain(_dot_nt(q[:, sl].astype(BF16), kh) * (MEM_HD ** -0.5))
        o_ref[:, sl] = _dot(p.astype(BF16), vh)


def _xattn_prompt_kernel(q_ref, kv_ref, o_ref):
    _xattn_heads(q_ref[...], kv_ref[...], o_ref)


def _xattn_sample_kernel(q_ref, kv_ref, buf_ref, o_ref):
    del buf_ref
    _xattn_heads(q_ref[...], kv_ref[0, 0], o_ref)


def cross_attention(qm, mp, kv_prompt, kv_cache, layer, *, ts, tm=512):
    m, width = qm.shape
    bs = (m - mp) // ts
    assert mp % tm == 0 and mp % ts == 0
    o = pl.pallas_call(
        _xattn_prompt_kernel,
        grid=(mp // tm,),
        in_specs=[pl.BlockSpec((tm, width), lambda i: (i, 0)),
                  pl.BlockSpec(kv_prompt.shape, lambda i: (0, 0))],
        out_specs=pl.BlockSpec((tm, width), lambda i: (i, 0)),
        out_shape=jax.ShapeDtypeStruct((m, width), F32),
        compiler_params=_params(("parallel",)),
        name="xattn_prompt",
    )(qm, kv_prompt)
    rb = mp // ts
    return pl.pallas_call(
        _xattn_sample_kernel,
        grid=(bs,),
        in_specs=[pl.BlockSpec((ts, width), lambda b: (rb + b, 0)),
                  pl.BlockSpec((1, 1) + kv_cache.shape[2:], lambda b: (layer, b, 0, 0)),
                  pl.BlockSpec(memory_space=pl.ANY)],
        out_specs=pl.BlockSpec((ts, width), lambda b: (rb + b, 0)),
        out_shape=jax.ShapeDtypeStruct((m, width), F32),
        input_output_aliases={2: 0},
        compiler_params=_params(("parallel",)),
        name="xattn_sample",
    )(qm, kv_cache, o)


def _gate_weight(w_in):
    per = N_GATE // NSA_HKV
    parts = [jnp.pad(w_in[:, N_MAIN0 + h * per:N_MAIN0 + (h + 1) * per], ((0, 0), (0, LANE - per)))
             for h in range(NSA_HKV)]
    return jnp.concatenate(parts, axis=1)


def kernel(x_prompt, x_sample, cache_nsa_kv, state_nsa_win_kv, state_lru_conv, state_lru_h, state_sconv, cache_mem_kv, page_table, mem_prompt, w_in0, lru_conv_w, lru_conv_b, lru_wx, lru_bx, lru_wa, lru_ba, lru_lambda, w_out0, w_in1, sconv_w, w_out1, w_q_mem, w_kv_mem, w_o_mem, w_gu, w_down, ln_g, ln_b):
    bp, tp = x_prompt.shape[:2]
    bs, ts = x_sample.shape[:2]
    assert bp == 1
    mp, ms = bp * tp, bs * ts
    past_len = page_table.shape[1] * PAGE_SIZE
    pos_all = jnp.concatenate([jnp.arange(tp, dtype=I32),
                               jnp.tile(past_len + jnp.arange(ts, dtype=I32), bs)])
    rope_cos, rope_sin = _rope_tables(pos_all)
    xa = jnp.concatenate([x_prompt.reshape(mp, D_MODEL), x_sample.reshape(ms, D_MODEL)], axis=0)
    p_nsa, p_win, p_conv, p_h, p_sc, p_mem = [], [], [], [], [], []
    s_nsa, s_win, s_conv, s_h, s_sc = [], [], [], [], []
    for l in range(DEPTH):
        if l % 2 == 0:
            e = l // 2
            lp = (lru_conv_w[e], lru_conv_b[e], lru_wx[e], lru_bx[e], lru_wa[e], lru_ba[e], lru_lambda[e])
            proj = matmul(xa, w_in0[e], n_cols=N_MAIN0)
            gt = matmul(xa, _gate_weight(w_in0[e]))
            q_bf, kv4, kv_bf, kvw, kw_bf, gates, ch = nsa_prep(proj, gt, rope_cos, rope_sin)
            y_lru, h_p, h_s = rglru(proj, mp, state_lru_conv[e], state_lru_h[e], *lp, bs=bs, ts=ts)
            nq, nkv, nkw = NSA_HQ * NSA_HD, 4 * NSA_KVW, 2 * NSA_KVW
            n_pool = cache_nsa_kv.shape[1]
            o_nsa = nsa_prompt(q_bf, gates, ch, kv_bf, kw_bf, tp)
            o_nsa, win_s = nsa_sample(q_bf[mp:].reshape(bs, ts, nq), gates, kv4, kvw, mp,
                                      cache_nsa_kv.reshape(-1, PAGE_SIZE, nkv), e * n_pool, page_table,
                                      state_nsa_win_kv.reshape(-1, state_nsa_win_kv.shape[2], nkw), e * bs,
                                      o_nsa)
            mix = jnp.concatenate([y_lru, o_nsa], axis=1)
            xb_s = proj[mp:, :W_LRU].reshape(bs, ts, W_LRU)
            p_nsa.append(kv4[:mp].reshape(bp, tp, 4, NSA_HKV, NSA_HD))
            p_win.append(kvw[mp - min(WINDOW, tp):mp].reshape(bp, -1, 2, NSA_HKV, NSA_HD))
            p_conv.append(proj[mp - (LRU_CONV - 1):mp, :W_LRU].reshape(bp, LRU_CONV - 1, W_LRU))
            p_h.append(h_p)
            s_nsa.append(kv4[mp:].reshape(bs, ts, 4, NSA_HKV, NSA_HD))
            s_win.append(win_s.reshape(state_nsa_win_kv.shape[1:]))
            s_conv.append(xb_s[:, ts - (LRU_CONV - 1):])
            s_h.append(h_s.reshape(bs, ts, W_LRU)[:, ts - 1])
            w_out = w_out0[e]
        else:
            o = l // 2
            h3 = matmul(xa, w_in1[o])
            mix, z = short_conv(h3, mp, state_sconv[o], sconv_w[o], ts=ts)
            p_sc.append(z[mp - (SC_CONV - 1):mp].reshape(bp, SC_CONV - 1, D_MODEL))
            s_sc.append(z[mp:].reshape(bs, ts, D_MODEL)[:, ts - (SC_CONV - 1):])
            w_out = w_out1[o]
        xa = matmul_postnorm(mix, w_out, xa, ln_g[l, 0], ln_b[l, 0])
        kv_mem_p = matmul(mem_prompt.reshape(bp * N_MEM, D_MODEL), w_kv_mem[l])
        p_mem.append(kv_mem_p.reshape(bp, N_MEM, 2, MEM_HEADS, MEM_HD))
        qm = matmul(xa, w_q_mem[l])
        om = cross_attention(qm, mp, kv_mem_p, cache_mem_kv.reshape(DEPTH, bs, N_MEM, -1), l, ts=ts)
        xa = matmul_postnorm(om, w_o_mem[l], xa, ln_g[l, 1], ln_b[l, 1])
        hf = matmul_swiglu(xa, w_gu[l])
        xa = matmul_postnorm(hf, w_down[l], xa, ln_g[l, 2], ln_b[l, 2])
    return (xa[:mp].reshape(bp, tp, D_MODEL), xa[mp:].reshape(bs, ts, D_MODEL),
            jnp.stack(p_nsa), jnp.stack(p_win), jnp.stack(p_conv), jnp.stack(p_h), jnp.stack(p_sc),
            jnp.stack(p_mem), jnp.stack(s_nsa), jnp.stack(s_win), jnp.stack(s_conv), jnp.stack(s_h),
            jnp.stack(s_sc))
```

```python
import functools

import numpy as np
import jax
import jax.numpy as jnp
from jax import lax
from jax.experimental import pallas as pl
from jax.experimental.pallas import tpu as pltpu

D_MODEL = 2048
SEQ = 8192
DEPTH = 2
DEC_BATCH = 128
DEC_SEQ = 8
PAGE_SIZE = 128

ALPHA = (2.0 * DEPTH) ** 0.25
LN_EPS = 1e-5
W_LRU = D_MODEL // 2
LRU_CONV = 4
LRU_BLOCKS = 8
LRU_BLOCK = W_LRU // LRU_BLOCKS
LRU_C = 8.0
NSA_HQ = 8
NSA_HKV = 2
NSA_HD = 128
NSA_G = NSA_HQ // NSA_HKV
NSA_KVW = NSA_HKV * NSA_HD
CMP_STRIDE = 16
CMP_LEN = 2 * CMP_STRIDE
SEL_BLOCK = 64
SEL_TOPK = 16
WINDOW = 512
Q_BLOCK = 128
ROPE_THETA = 10000.0
BIG = 1e6
NEG = -1e30
NSA_SCALE = NSA_HD ** -0.5
EVEN_SPLITS = [W_LRU, 2 * W_LRU, 2 * W_LRU + NSA_HQ * NSA_HD,
               2 * W_LRU + NSA_HQ * NSA_HD + 4 * NSA_KVW,
               2 * W_LRU + NSA_HQ * NSA_HD + 6 * NSA_KVW]
N_MAIN0 = EVEN_SPLITS[-1]
N_GATE = 3 * NSA_HQ
SC_CONV = 3
N_MEM = 256
MEM_HEADS = 4
MEM_HD = D_MODEL // MEM_HEADS
D_FF = ((8 * D_MODEL + 3 * 256 - 1) // (3 * 256)) * 256

LANE = 128
VMEM_LIMIT = 56 * 1024 * 1024
SEL_TILE = 512
MM_TILE_M = 1024
MM_TILE_N = 1024
MM_MAX_K = 2048
LN_TILE_M = 512
LN_K_SPLIT = 4

F32 = jnp.float32
BF16 = jnp.bfloat16
I32 = jnp.int32


def _params(sem):
    return pltpu.CompilerParams(dimension_semantics=sem, vmem_limit_bytes=VMEM_LIMIT)


def _dot(a, b):
    return jnp.dot(a, b, preferred_element_type=F32)


def _dot_nt(a, b):
    return lax.dot_general(a, b, (((1,), (1,)), ((), ())), preferred_element_type=F32)


def _mm_kernel(x_ref, w_ref, o_ref):
    o_ref[...] = _dot(x_ref[...], w_ref[...]).astype(o_ref.dtype)


def _pick_tile(n, prefs):
    for t in prefs:
        if n % t == 0:
            return t
    return n


def matmul(x, w, *, n_cols=None, out_dtype=F32):
    m, kdim = x.shape
    n = w.shape[1] if n_cols is None else n_cols
    assert x.dtype == BF16 and w.dtype == BF16
    tm = _pick_tile(m, (MM_TILE_M,))
    tn = _pick_tile(n, (MM_TILE_N, MM_TILE_N // 2, MM_TILE_N // 4))
    return pl.pallas_call(
        _mm_kernel,
        grid=(m // tm, n // tn),
        in_specs=[pl.BlockSpec((tm, kdim), lambda i, j: (i, 0)),
                  pl.BlockSpec((kdim, tn), lambda i, j: (0, j))],
        out_specs=pl.BlockSpec((tm, tn), lambda i, j: (i, j)),
        out_shape=jax.ShapeDtypeStruct((m, n), out_dtype),
        compiler_params=_params(("parallel", "parallel")),
        name="matmul",
    )(x, w)


def _mm_ln_kernel(x_ref, w_ref, res_ref, g_ref, b_ref, o_ref, obf_ref, *acc, nk):
    part = _dot(x_ref[...], w_ref[...])

    def finish(y):
        z = ALPHA * res_ref[...] + y
        mu = jnp.mean(z, axis=-1, keepdims=True)
        zc = z - mu
        var = jnp.mean(zc * zc, axis=-1, keepdims=True)
        out = zc * lax.rsqrt(var + LN_EPS) * g_ref[...] + b_ref[...]
        o_ref[...] = out
        obf_ref[...] = out.astype(BF16)

    if nk == 1:
        finish(part)
        return
    acc_ref, = acc
    k = pl.program_id(1)

    @pl.when(k == 0)
    def _():
        acc_ref[...] = part

    @pl.when(k > 0)
    def _():
        acc_ref[...] += part

    @pl.when(k == nk - 1)
    def _():
        finish(acc_ref[...])


def matmul_postnorm(x, w, res, g, b):
    m, kdim = x.shape
    n = w.shape[1]
    assert x.dtype == BF16 and w.dtype == BF16 and res.shape == (m, n)
    tm = _pick_tile(m, (LN_TILE_M,))
    nk = 1 if kdim <= MM_MAX_K else LN_K_SPLIT
    assert kdim % (nk * LANE) == 0
    tk = kdim // nk
    return pl.pallas_call(
        functools.partial(_mm_ln_kernel, nk=nk),
        grid=(m // tm, nk),
        in_specs=[pl.BlockSpec((tm, tk), lambda i, k: (i, k)),
                  pl.BlockSpec((tk, n), lambda i, k: (k, 0)),
                  pl.BlockSpec((tm, n), lambda i, k: (i, 0)),
                  pl.BlockSpec((1, n), lambda i, k: (0, 0)),
                  pl.BlockSpec((1, n), lambda i, k: (0, 0))],
        out_specs=[pl.BlockSpec((tm, n), lambda i, k: (i, 0)), pl.BlockSpec((tm, n), lambda i, k: (i, 0))],
        out_shape=[jax.ShapeDtypeStruct((m, n), F32), jax.ShapeDtypeStruct((m, n), BF16)],
        scratch_shapes=[] if nk == 1 else [pltpu.VMEM((tm, n), F32)],
        compiler_params=_params(("parallel", "arbitrary")),
        name="matmul_postnorm",
    )(x, w, res, g.reshape(1, n), b.reshape(1, n))


def _mm_swiglu_kernel(x_ref, wg_ref, wu_ref, o_ref):
    x = x_ref[...]
    g = _dot(x, wg_ref[...])
    u = _dot(x, wu_ref[...])
    o_ref[...] = (g * jax.nn.sigmoid(g) * u).astype(o_ref.dtype)


def matmul_swiglu(x, w_gu):
    m, kdim = x.shape
    f = w_gu.shape[1] // 2
    assert x.dtype == BF16 and w_gu.dtype == BF16
    tm = _pick_tile(m, (MM_TILE_M,))
    tn = _pick_tile(f, (MM_TILE_N // 2, MM_TILE_N // 4))
    nj = f // tn
    return pl.pallas_call(
        _mm_swiglu_kernel,
        grid=(m // tm, nj),
        in_specs=[pl.BlockSpec((tm, kdim), lambda i, j: (i, 0)),
                  pl.BlockSpec((kdim, tn), lambda i, j: (0, j)),
                  pl.BlockSpec((kdim, tn), lambda i, j: (0, j + nj))],
        out_specs=pl.BlockSpec((tm, tn), lambda i, j: (i, j)),
        out_shape=jax.ShapeDtypeStruct((m, f), BF16),
        compiler_params=_params(("parallel", "parallel")),
        name="matmul_swiglu",
    )(x, w_gu, w_gu)


def _rope_tables(pos):
    half = NSA_HD // 2
    inv = ROPE_THETA ** (-jnp.arange(half, dtype=F32) / half)
    ang = pos.astype(F32)[:, None] * inv[None, :]
    cos, sin = jnp.cos(ang), jnp.sin(ang)
    return jnp.concatenate([cos, cos], axis=-1), jnp.concatenate([-sin, sin], axis=-1)


def _nsa_prep_kernel(q_ref, kv_ref, kw_ref, gt_ref, cos_ref, sin_ref,
                     qbf_ref, kv_o_ref, kvbf_ref, kw_o_ref, kwbf_ref, g_o_ref, ch_ref):
    cos = cos_ref[...]
    sin = sin_ref[...]
    rows = q_ref.shape[0]

    def rot(x):
        return x * cos + pltpu.roll(x, NSA_HD // 2, axis=1) * sin

    for h in range(NSA_HQ):
        sl = slice(NSA_HD * h, NSA_HD * (h + 1))
        qbf_ref[:, sl] = rot(q_ref[:, sl]).astype(BF16)
    for c in range(4 * NSA_HKV):
        sl = slice(NSA_HD * c, NSA_HD * (c + 1))
        slot = c // NSA_HKV
        x = kv_ref[:, sl]
        if slot in (0, 2):
            x = rot(x)
        kv_o_ref[pl.ds(c, rows, stride=4 * NSA_HKV), :] = x
        kvbf_ref[:, sl] = x.astype(BF16)
        if slot < 2:
            ch_ref[:, sl] = jnp.sum(x.reshape(rows // CMP_STRIDE, CMP_STRIDE, NSA_HD), axis=1)
    for c in range(2 * NSA_HKV):
        sl = slice(NSA_HD * c, NSA_HD * (c + 1))
        x = kw_ref[:, sl]
        if c // NSA_HKV == 0:
            x = rot(x)
        kw_o_ref[pl.ds(c, rows, stride=2 * NSA_HKV), :] = x
        kwbf_ref[:, sl] = x.astype(BF16)
    g_o_ref[...] = jax.nn.sigmoid(gt_ref[...])


def nsa_prep(proj, gt, cos, sin, *, tm=512):
    m = proj.shape[0]
    assert m % tm == 0
    qc = EVEN_SPLITS[1] // (NSA_HQ * NSA_HD)
    kc = EVEN_SPLITS[2] // (4 * NSA_KVW)
    wc = EVEN_SPLITS[3] // (2 * NSA_KVW)
    nq, nkv, nkw, ng = NSA_HQ * NSA_HD, 4 * NSA_KVW, 2 * NSA_KVW, NSA_HKV * LANE
    row = lambda n: pl.BlockSpec((tm, n), lambda i: (i, 0))
    return pl.pallas_call(
        _nsa_prep_kernel,
        grid=(m // tm,),
        in_specs=[pl.BlockSpec((tm, nq), lambda i: (i, qc)),
                  pl.BlockSpec((tm, nkv), lambda i: (i, kc)),
                  pl.BlockSpec((tm, nkw), lambda i: (i, wc)),
                  row(ng), row(NSA_HD), row(NSA_HD)],
        out_specs=[row(nq), pl.BlockSpec((tm * nkv // NSA_HD, NSA_HD), lambda i: (i, 0)), row(nkv),
                   pl.BlockSpec((tm * nkw // NSA_HD, NSA_HD), lambda i: (i, 0)), row(nkw), row(ng),
                   pl.BlockSpec((tm // CMP_STRIDE, 2 * NSA_KVW), lambda i: (i, 0))],
        out_shape=[jax.ShapeDtypeStruct((m, nq), BF16),
                   jax.ShapeDtypeStruct((m * nkv // NSA_HD, NSA_HD), F32),
                   jax.ShapeDtypeStruct((m, nkv), BF16),
                   jax.ShapeDtypeStruct((m * nkw // NSA_HD, NSA_HD), F32),
                   jax.ShapeDtypeStruct((m, nkw), BF16),
                   jax.ShapeDtypeStruct((m, ng), F32),
                   jax.ShapeDtypeStruct((m // CMP_STRIDE, 2 * NSA_KVW), F32)],
        compiler_params=_params(("parallel",)),
        name="nsa_prep",
    )(proj, proj, proj, gt, cos, sin)


def _softmax_rows(s, mask):
    sm = jnp.where(mask, s, NEG)
    e = jnp.where(mask, jnp.exp(sm - jnp.max(sm, axis=-1, keepdims=True)), 0.0)
    return e / jnp.maximum(jnp.sum(e, axis=-1, keepdims=True), 1e-30)


def _block_importance(imp, pmat):
    hi = imp.astype(BF16)
    r1 = imp - hi.astype(F32)
    mid = r1.astype(BF16)
    lo = (r1 - mid.astype(F32)).astype(BF16)
    return _dot(hi, pmat) + _dot(mid, pmat) + _dot(lo, pmat)


def _select_blocks(imp_sel, posq):
    blk = lax.broadcasted_iota(I32, (1, LANE), 1)
    cur = posq // SEL_BLOCK
    forced = (blk == 0) | (blk == cur) | (blk == cur - 1)
    valid = blk * SEL_BLOCK <= posq
    score = jnp.where(valid, jnp.where(forced, BIG, imp_sel), -BIG)
    key = lax.bitcast_convert_type(score, I32)
    key = jnp.where(key < 0, key ^ 0x7FFFFFFF, key)
    key_m1 = key - 1
    cnt = jnp.zeros(score.shape, I32)
    for i in range(LANE):
        thr = jnp.where(blk > i, key_m1, key)
        cnt = cnt + (key[:, i:i + 1] > thr).astype(I32)
    return (cnt < SEL_TOPK).astype(F32)


def _pool_matrix(n_cmp):
    c = np.arange(n_cmp)[:, None]
    j = np.arange(LANE)[None, :]
    return jnp.asarray(((c // 4 == j).astype(np.float32) + ((c + 1) // 4 == j)), dtype=BF16)


def _nsa_prompt_kernel(q_ref, g_ref, chk_ref, chv_ref, ks_ref, vs_ref, kw_ref, vw_ref, pmat_ref,
                       o_ref, kct_ref, vc_ref):
    qb = pl.program_id(1)
    n_cmp = chk_ref.shape[0]

    @pl.when(qb == 0)
    def _():
        ck = chk_ref[...]
        kc = (ck + pltpu.roll(ck, n_cmp - 1, axis=0)) * (1.0 / CMP_LEN)
        kct_ref[...] = kc.T.astype(BF16)
        cv = chv_ref[...]
        vc_ref[...] = ((cv + pltpu.roll(cv, n_cmp - 1, axis=0)) * (1.0 / CMP_LEN)).astype(BF16)

    qblk = q_ref[...]
    q2 = jnp.concatenate([qblk[:, NSA_HD * g:NSA_HD * (g + 1)] for g in range(NSA_G)], axis=0)
    rows = NSA_G * Q_BLOCK
    posq = qb * Q_BLOCK + lax.broadcasted_iota(I32, (Q_BLOCK, 1), 0)
    posq4 = jnp.concatenate([posq] * NSA_G, axis=0)

    s = _dot(q2, kct_ref[...]) * NSA_SCALE
    cend = lax.broadcasted_iota(I32, (1, n_cmp), 1) * CMP_STRIDE + (CMP_LEN - 1)
    p = _softmax_rows(s, cend <= posq4)
    o_cmp = _dot(p.astype(BF16), vc_ref[...])
    imp = p[0:Q_BLOCK]
    for g in range(1, NSA_G):
        imp = imp + p[g * Q_BLOCK:(g + 1) * Q_BLOCK]
    sel = _select_blocks(_block_importance(imp, pmat_ref[...]), posq).astype(BF16)

    brow = lax.broadcasted_iota(I32, (LANE, 1), 0)
    kcol = lax.broadcasted_iota(I32, (1, SEL_TILE), 1)
    kdiv = kcol // SEL_BLOCK

    def body(t, carry):
        m, l, acc = carry
        off = pl.multiple_of(t * SEL_TILE, SEL_TILE)
        k = ks_ref[pl.ds(off, SEL_TILE), :]
        v = vs_ref[pl.ds(off, SEL_TILE), :]
        st = _dot_nt(q2, k) * NSA_SCALE
        expand = (brow == kdiv + t * (SEL_TILE // SEL_BLOCK)).astype(BF16)
        chosen = _dot(sel, expand)
        msk = (chosen > 0.5) & (kcol + t * SEL_TILE <= posq)
        sm = st.reshape(NSA_G, Q_BLOCK, SEL_TILE) + jnp.where(msk, 0.0, NEG)[None]
        m_new = jnp.maximum(m, jnp.max(sm, axis=-1, keepdims=True))
        alpha = jnp.exp(m - m_new)
        e = jnp.exp(sm - m_new)
        l = alpha * l + jnp.sum(e, axis=-1, keepdims=True)
        pv = _dot(e.reshape(rows, SEL_TILE).astype(BF16), v)
        acc = alpha.reshape(rows, 1) * acc + pv
        return m_new, l, acc

    m0 = jnp.full((NSA_G, Q_BLOCK, 1), NEG, F32)
    l0 = jnp.zeros((NSA_G, Q_BLOCK, 1), F32)
    n_tiles = qb // (SEL_TILE // Q_BLOCK) + 1
    _, l, acc = lax.fori_loop(0, n_tiles, body, (m0, l0, jnp.zeros((rows, NSA_HD), F32)))
    o_sel = acc / jnp.maximum(l.reshape(rows, 1), 1e-30)

    span = WINDOW + Q_BLOCK
    start = pl.multiple_of(jnp.maximum(qb - WINDOW // Q_BLOCK, 0) * Q_BLOCK, Q_BLOCK)
    kw = kw_ref[pl.ds(start, span), :]
    vw = vw_ref[pl.ds(start, span), :]
    s = _dot_nt(q2, kw) * NSA_SCALE
    dpos = posq4 - (start + lax.broadcasted_iota(I32, (1, span), 1))
    p = _softmax_rows(s, (dpos >= 0) & (dpos < WINDOW))
    o_win = _dot(p.astype(BF16), vw)

    gates = g_ref[...]
    for g in range(NSA_G):
        rs = slice(g * Q_BLOCK, (g + 1) * Q_BLOCK)
        o_ref[:, NSA_HD * g:NSA_HD * (g + 1)] = (gates[:, 3 * g:3 * g + 1] * o_cmp[rs]
                                                + gates[:, 3 * g + 1:3 * g + 2] * o_sel[rs]
                                                + gates[:, 3 * g + 2:3 * g + 3] * o_win[rs])


def nsa_prompt(q_bf, gates, ch, kv_bf, kw_bf, t):
    n_qb = t // Q_BLOCK
    n_cmp = t // CMP_STRIDE
    gw = NSA_G * NSA_HD
    res = lambda c0: pl.BlockSpec((t, NSA_HD), lambda h, i: (0, c0 + h))
    return pl.pallas_call(
        _nsa_prompt_kernel,
        grid=(NSA_HKV, n_qb),
        in_specs=[pl.BlockSpec((Q_BLOCK, gw), lambda h, i: (i, h)),
                  pl.BlockSpec((Q_BLOCK, LANE), lambda h, i: (i, h)),
                  pl.BlockSpec((n_cmp, NSA_HD), lambda h, i: (0, h)),
                  pl.BlockSpec((n_cmp, NSA_HD), lambda h, i: (0, NSA_HKV + h)),
                  res(2 * NSA_HKV), res(3 * NSA_HKV), res(0), res(NSA_HKV),
                  pl.BlockSpec((n_cmp, LANE), lambda h, i: (0, 0))],
        out_specs=pl.BlockSpec((Q_BLOCK, gw), lambda h, i: (i, h)),
        out_shape=jax.ShapeDtypeStruct((q_bf.shape[0], NSA_HQ * NSA_HD), F32),
        scratch_shapes=[pltpu.VMEM((NSA_HD, n_cmp), BF16), pltpu.VMEM((n_cmp, NSA_HD), BF16)],
        compiler_params=_params(("parallel", "arbitrary")),
        name="nsa_prompt",
    )(q_bf, gates, ch, ch, kv_bf, kv_bf, kw_bf, kw_bf, _pool_matrix(n_cmp))


def _nsa_sample_kernel(pt_ref, q_ref, g_ref, kvn_ref, kwn_ref, win_ref, *rest, n_pages, past_len):
    pages = rest[:n_pages]
    pmat_ref, emat_ref, _, o_ref, nwin_ref = rest[n_pages:]
    tq = q_ref.shape[1]
    n_kv, n_kw = 4 * NSA_HKV, 2 * NSA_HKV
    w_buf = win_ref.shape[0] // n_kw
    pad_rows = PAGE_SIZE - tq
    qf = q_ref[0].astype(F32)
    gates = g_ref[...]

    def rows_of(ref, c, n, per):
        return ref[pl.ds(c, n, stride=per), :]
    posq = past_len + lax.broadcasted_iota(I32, (tq, 1), 0)
    posq4 = jnp.concatenate([posq] * NSA_G, axis=0)
    zpad = jnp.zeros((pad_rows, NSA_HD), F32)

    def new_page(x):
        return jnp.concatenate([x, zpad], axis=0).astype(BF16)

    def chunk_sums(x):
        return jnp.sum(x.reshape(x.shape[0] // CMP_STRIDE, CMP_STRIDE, NSA_HD), axis=1)

    n_cmp = past_len // CMP_STRIDE
    crow = lax.broadcasted_iota(I32, (n_cmp, 1), 0)

    def compressed(c):
        ch = jnp.concatenate([chunk_sums(rows_of(pg, c, PAGE_SIZE, n_kv)) for pg in pages], axis=0)
        ch_new = jnp.sum(rows_of(kvn_ref, c, tq, n_kv), axis=0, keepdims=True)
        nxt = jnp.where(crow == n_cmp - 1, ch_new, pltpu.roll(ch, n_cmp - 1, axis=0))
        return ((ch + nxt) * (1.0 / CMP_LEN)).astype(BF16)

    for h in range(NSA_HKV):
        q2 = jnp.concatenate([qf[:, NSA_HD * (NSA_G * h + g):NSA_HD * (NSA_G * h + g + 1)]
                              for g in range(NSA_G)], axis=0).astype(BF16)
        col = lambda slot: slot * NSA_HKV + h

        kc = compressed(col(0))
        vc = compressed(col(1))
        s = _dot_nt(q2, kc) * NSA_SCALE
        cend = lax.broadcasted_iota(I32, (1, n_cmp), 1) * CMP_STRIDE + (CMP_LEN - 1)
        p = _softmax_rows(s, cend <= posq4)
        o_cmp = _dot(p.astype(BF16), vc)
        imp = p[0:tq]
        for g in range(1, NSA_G):
            imp = imp + p[g * tq:(g + 1) * tq]
        sel = _select_blocks(_block_importance(imp, pmat_ref[...]), posq)
        sel4 = jnp.concatenate([sel] * NSA_G, axis=0).astype(BF16)

        chosen = _dot(sel4, emat_ref[...])
        pieces = [_dot_nt(q2, rows_of(pg, col(2), PAGE_SIZE, n_kv).astype(BF16)) for pg in pages]
        pieces.append(_dot_nt(q2, new_page(rows_of(kvn_ref, col(2), tq, n_kv))))
        s = jnp.concatenate(pieces, axis=1) * NSA_SCALE
        n_key = s.shape[1]
        kpos = lax.broadcasted_iota(I32, (1, n_key), 1)
        p = _softmax_rows(s, (chosen > 0.5) & (kpos <= posq4)).astype(BF16)
        o_sel = _dot(p[:, past_len:], new_page(rows_of(kvn_ref, col(3), tq, n_kv)))
        for i, pg in enumerate(pages):
            o_sel = o_sel + _dot(p[:, i * PAGE_SIZE:(i + 1) * PAGE_SIZE],
                                 rows_of(pg, col(3), PAGE_SIZE, n_kv).astype(BF16))

        kc_w, vc_w = h, NSA_HKV + h
        s = jnp.concatenate([_dot_nt(q2, rows_of(win_ref, kc_w, w_buf, n_kw).astype(BF16)),
                             _dot_nt(q2, new_page(rows_of(kwn_ref, kc_w, tq, n_kw)))], axis=1) * NSA_SCALE
        posw = (past_len - w_buf) + lax.broadcasted_iota(I32, (1, w_buf + PAGE_SIZE), 1)
        dpos = posq4 - posw
        p = _softmax_rows(s, (dpos >= 0) & (dpos < WINDOW) & (posw >= 0)).astype(BF16)
        o_win = (_dot(p[:, :w_buf], rows_of(win_ref, vc_w, w_buf, n_kw).astype(BF16))
                 + _dot(p[:, w_buf:], new_page(rows_of(kwn_ref, vc_w, tq, n_kw))))

        for g in range(NSA_G):
            rs = slice(g * tq, (g + 1) * tq)
            gc = h * LANE + 3 * g
            head = NSA_G * h + g
            o_ref[:, NSA_HD * head:NSA_HD * (head + 1)] = (
                gates[:, gc:gc + 1] * o_cmp[rs] + gates[:, gc + 1:gc + 2] * o_sel[rs]
                + gates[:, gc + 2:gc + 3] * o_win[rs])

    nwin_ref[0:(w_buf - tq) * n_kw, :] = win_ref[tq * n_kw:w_buf * n_kw, :]
    nwin_ref[(w_buf - tq) * n_kw:w_buf * n_kw, :] = kwn_ref[...]


def nsa_sample(q_bf, gates, kv_new, kw_new, mp, pool, page_base, page_table, win_all, win_base, w_buf, o_buf):
    bsz, tq, nq = q_bf.shape
    n_pages = page_table.shape[1]
    past_len = n_pages * PAGE_SIZE
    n_kv, n_kw = 4 * NSA_HKV, 2 * NSA_HKV
    assert tq == 8 and past_len % SEL_BLOCK == 0 and mp % tq == 0
    n_cmp = past_len // CMP_STRIDE
    n_key = past_len + PAGE_SIZE
    rb = mp // tq
    emat = jnp.asarray(np.arange(LANE)[:, None] == (np.arange(n_key)[None, :] // SEL_BLOCK), dtype=BF16)
    seq = lambda r, n: pl.BlockSpec((r, n), lambda b, pt: (rb + b, 0))
    page = lambda p: pl.BlockSpec((PAGE_SIZE * n_kv, NSA_HD),
                                  lambda b, pt: (page_base + pt[b * n_pages + p], 0))
    const = lambda shape: pl.BlockSpec(shape, lambda b, pt: (0, 0))
    return pl.pallas_call(
        functools.partial(_nsa_sample_kernel, n_pages=n_pages, past_len=past_len),
        grid_spec=pltpu.PrefetchScalarGridSpec(
            num_scalar_prefetch=1,
            grid=(bsz,),
            in_specs=[pl.BlockSpec((1, tq, nq), lambda b, pt: (b, 0, 0)),
                      seq(tq, gates.shape[1]), seq(tq * n_kv, NSA_HD), seq(tq * n_kw, NSA_HD),
                      pl.BlockSpec((w_buf * n_kw, NSA_HD), lambda b, pt: (win_base + b, 0))]
                     + [page(p) for p in range(n_pages)]
                     + [const((n_cmp, LANE)), const((LANE, n_key)), pl.BlockSpec(memory_space=pl.ANY)],
            out_specs=[seq(tq, nq), pl.BlockSpec((w_buf * n_kw, NSA_HD), lambda b, pt: (b, 0))]),
        out_shape=[jax.ShapeDtypeStruct(o_buf.shape, F32),
                   jax.ShapeDtypeStruct((bsz * w_buf * n_kw, NSA_HD), F32)],
        input_output_aliases={8 + n_pages: 0},
        compiler_params=_params(("parallel",)),
        name="nsa_sample",
    )(page_table.reshape(-1), q_bf, gates, kv_new, kw_new, win_all, *([pool] * n_pages),
      _pool_matrix(n_cmp), emat, o_buf)


def _shift_rows(x, s):
    return pltpu.roll(x, s % x.shape[0], axis=0)


def _lru_recurrence_inputs(xc, wx_ref, bx_ref, wa_ref, ba_ref, lam_ref):
    xb = xc.astype(BF16)
    i_g = jax.nn.sigmoid(_dot(xb, wx_ref[0].astype(BF16)) + bx_ref[...])
    r_g = jax.nn.sigmoid(_dot(xb, wa_ref[0].astype(BF16)) + ba_ref[...])
    z = -lam_ref[...]
    softplus = jnp.maximum(z, 0.0) + jnp.log1p(jnp.exp(-jnp.abs(z)))
    log_a = -LRU_C * r_g * softplus
    a = jnp.exp(log_a)
    th = jnp.tanh(log_a)
    u = jnp.sqrt(-2.0 * th / (1.0 - th)) * (i_g * xc)
    return a, u


def _scan_rows(a, u, tpos, length):
    d = 1
    while d < length:
        keep = tpos >= d
        a_prev = jnp.where(keep, _shift_rows(a, d), 1.0)
        u_prev = jnp.where(keep, _shift_rows(u, d), 0.0)
        u = a * u_prev + u
        a = a * a_prev
        d *= 2
    return a, u


def _lru_prompt_kernel(x_ref, g_ref, cw_ref, cb_ref, wx_ref, bx_ref, wa_ref, ba_ref, lam_ref,
                       y_ref, hl_ref, tail_ref, hs_ref):
    tt = pl.program_id(1)
    rows = x_ref.shape[0]

    @pl.when(tt == 0)
    def _():
        tail_ref[...] = jnp.zeros(tail_ref.shape, F32)
        hs_ref[...] = jnp.zeros(hs_ref.shape, F32)

    x = x_ref[...]
    w = cw_ref[...]
    tail = tail_ref[...]
    row8 = lax.broadcasted_iota(I32, (8, LRU_BLOCK), 0)
    acc = None
    for i in range(LRU_CONV):
        s = LRU_CONV - 1 - i
        if s == 0:
            xs = x
        else:
            xs = _shift_rows(x, s)
            head = jnp.where(row8 < s, _shift_rows(tail, s), xs[0:8])
            xs = jnp.concatenate([head, xs[8:]], axis=0)
        term = w[i:i + 1] * xs
        acc = term if acc is None else acc + term
    xc = acc + cb_ref[...]
    tail_ref[...] = x[rows - 8:rows]

    a, u = _lru_recurrence_inputs(xc, wx_ref, bx_ref, wa_ref, ba_ref, lam_ref)
    tpos = lax.broadcasted_iota(I32, (rows, 1), 0)
    a, u = _scan_rows(a, u, tpos, rows)
    h = u + a * hs_ref[7:8, :]
    hs_ref[...] = h[rows - 8:rows]
    hl_ref[...] = h[rows - 1:rows]
    y_ref[...] = h * jax.nn.gelu(g_ref[...])


def _lru_sample_kernel(x_ref, g_ref, buf_ref, h0_ref, cw_ref, cb_ref, wx_ref, bx_ref, wa_ref, ba_ref,
                       lam_ref, mix_ref, y_ref, h_ref, *, tq):
    del mix_ref
    rows = x_ref.shape[0]
    x = x_ref[...]
    w = cw_ref[...]
    buf = buf_ref[...]
    tpos = lax.broadcasted_iota(I32, (rows, 1), 0) % tq
    acc = None
    for i in range(LRU_CONV):
        s = LRU_CONV - 1 - i
        xs = x if s == 0 else jnp.where(tpos >= s, _shift_rows(x, s), _shift_rows(buf, -i))
        term = w[i:i + 1] * xs
        acc = term if acc is None else acc + term
    xc = acc + cb_ref[...]
    a, u = _lru_recurrence_inputs(xc, wx_ref, bx_ref, wa_ref, ba_ref, lam_ref)
    a, u = _scan_rows(a, u, tpos, tq)
    h = u + a * h0_ref[...]
    h_ref[...] = h
    y_ref[...] = h * jax.nn.gelu(g_ref[...])


def rglru(proj, mp, conv_state, h0, conv_w, conv_b, wx, bx, wa, ba, lam, *, bs, ts, tt=512):
    m = proj.shape[0]
    ms = bs * ts
    nb = LRU_BLOCKS
    assert mp % tt == 0 and mp % ms == 0 and ts == 8
    vec = lambda v: v.reshape(1, W_LRU)
    cspec = lambda shape, im: pl.BlockSpec(shape, im)
    par_p = [cspec((LRU_CONV, LRU_BLOCK), lambda c, t: (0, c)), cspec((1, LRU_BLOCK), lambda c, t: (0, c)),
             cspec((1, LRU_BLOCK, LRU_BLOCK), lambda c, t: (c, 0, 0)), cspec((1, LRU_BLOCK), lambda c, t: (0, c)),
             cspec((1, LRU_BLOCK, LRU_BLOCK), lambda c, t: (c, 0, 0)), cspec((1, LRU_BLOCK), lambda c, t: (0, c)),
             cspec((1, LRU_BLOCK), lambda c, t: (0, c))]
    params = (conv_w, vec(conv_b), wx, vec(bx), wa, vec(ba), vec(lam))
    y, h_last = pl.pallas_call(
        _lru_prompt_kernel,
        grid=(nb, mp // tt),
        in_specs=[pl.BlockSpec((tt, LRU_BLOCK), lambda c, t: (t, c)),
                  pl.BlockSpec((tt, LRU_BLOCK), lambda c, t: (t, nb + c))] + par_p,
        out_specs=[pl.BlockSpec((tt, LRU_BLOCK), lambda c, t: (t, c)),
                   pl.BlockSpec((1, LRU_BLOCK), lambda c, t: (0, c))],
        out_shape=[jax.ShapeDtypeStruct((m, W_LRU), F32), jax.ShapeDtypeStruct((1, W_LRU), F32)],
        scratch_shapes=[pltpu.VMEM((8, LRU_BLOCK), F32), pltpu.VMEM((8, LRU_BLOCK), F32)],
        compiler_params=_params(("parallel", "arbitrary")),
        name="lru_prompt",
    )(proj, proj, *params)

    buf = jnp.pad(conv_state, ((0, 0), (0, ts - (LRU_CONV - 1)), (0, 0))).reshape(ms, W_LRU)
    h0r = jnp.repeat(h0, ts, axis=0)
    rb = mp // ms
    par_s = [cspec((LRU_CONV, LRU_BLOCK), lambda c: (0, c)), cspec((1, LRU_BLOCK), lambda c: (0, c)),
             cspec((1, LRU_BLOCK, LRU_BLOCK), lambda c: (c, 0, 0)), cspec((1, LRU_BLOCK), lambda c: (0, c)),
             cspec((1, LRU_BLOCK, LRU_BLOCK), lambda c: (c, 0, 0)), cspec((1, LRU_BLOCK), lambda c: (0, c)),
             cspec((1, LRU_BLOCK), lambda c: (0, c))]
    y, h_s = pl.pallas_call(
        functools.partial(_lru_sample_kernel, tq=ts),
        grid=(nb,),
        in_specs=[pl.BlockSpec((ms, LRU_BLOCK), lambda c: (rb, c)),
                  pl.BlockSpec((ms, LRU_BLOCK), lambda c: (rb, nb + c)),
                  pl.BlockSpec((ms, LRU_BLOCK), lambda c: (0, c)),
                  pl.BlockSpec((ms, LRU_BLOCK), lambda c: (0, c))] + par_s
                 + [pl.BlockSpec(memory_space=pl.ANY)],
        out_specs=[pl.BlockSpec((ms, LRU_BLOCK), lambda c: (rb, c)),
                   pl.BlockSpec((ms, LRU_BLOCK), lambda c: (0, c))],
        out_shape=[jax.ShapeDtypeStruct((m, W_LRU), F32), jax.ShapeDtypeStruct((ms, W_LRU), F32)],
        input_output_aliases={4 + len(par_s): 0},
        compiler_params=_params(("parallel",)),
        name="lru_sample",
    )(proj, proj, buf, h0r, *params, y)
    return y, h_last, h_s


def _sconv_kernel(h_ref, prev_ref, buf_ref, w_ref, o_ref, z_ref, *, n_prompt_tiles, tq):
    i = pl.program_id(0)
    rows = h_ref.shape[0]
    n = o_ref.shape[1]
    bg = h_ref[:, 0:n]
    z = h_ref[:, n:2 * n] * h_ref[:, 2 * n:3 * n]
    z_ref[...] = z
    w = w_ref[...]

    def finish(fix):
        acc = None
        for k in range(SC_CONV):
            s = SC_CONV - 1 - k
            zs = z if s == 0 else fix(_shift_rows(z, s), s, k)
            term = w[k:k + 1] * zs
            acc = term if acc is None else acc + term
        o_ref[...] = (bg * acc).astype(o_ref.dtype)

    @pl.when(i < n_prompt_tiles)
    def _():
        pz = prev_ref[:, n:2 * n] * prev_ref[:, 2 * n:3 * n]
        pz = jnp.where(i > 0, pz, 0.0)
        row8 = lax.broadcasted_iota(I32, (8, 1), 0)

        def fix(zs, s, k):
            head = jnp.where(row8 < s, _shift_rows(pz, s), zs[0:8])
            return jnp.concatenate([head, zs[8:]], axis=0)

        finish(fix)

    @pl.when(i >= n_prompt_tiles)
    def _():
        tpos = lax.broadcasted_iota(I32, (rows, 1), 0) % tq
        buf = buf_ref[...]
        finish(lambda zs, s, k: jnp.where(tpos >= s, zs, _shift_rows(buf, -k)))


def short_conv(h3, mp, conv_state, conv_w, *, ts, tm=256):
    m = h3.shape[0]
    n = h3.shape[1] // 3
    ms = m - mp
    assert mp % tm == 0 and ms % tm == 0 and ts == 8
    npt = mp // tm
    buf = jnp.pad(conv_state, ((0, 0), (0, ts - (SC_CONV - 1)), (0, 0))).reshape(ms, n)
    return pl.pallas_call(
        functools.partial(_sconv_kernel, n_prompt_tiles=npt, tq=ts),
        grid=(m // tm,),
        in_specs=[pl.BlockSpec((tm, 3 * n), lambda i: (i, 0)),
                  pl.BlockSpec((8, 3 * n), lambda i: (jnp.maximum(jnp.minimum(i, npt) * (tm // 8) - 1, 0), 0)),
                  pl.BlockSpec((tm, n), lambda i: (jnp.maximum(i - npt, 0), 0)),
                  pl.BlockSpec((SC_CONV, n), lambda i: (0, 0))],
        out_specs=[pl.BlockSpec((tm, n), lambda i: (i, 0)), pl.BlockSpec((tm, n), lambda i: (i, 0))],
        out_shape=[jax.ShapeDtypeStruct((m, n), BF16), jax.ShapeDtypeStruct((m, n), F32)],
        compiler_params=_params(("parallel",)),
        name="short_conv",
    )(h3, h3, buf, conv_w)


def _softmax_plain(s):
    e = jnp.exp(s - jnp.max(s, axis=-1, keepdims=True))
    return e / jnp.sum(e, axis=-1, keepdims=True)


def _xattn_heads(q, kv, o_ref):
    width = MEM_HEADS * MEM_HD
    for h in range(MEM_HEADS):
        sl = slice(h * MEM_HD, (h + 1) * MEM_HD)
        kh = kv[:, sl].astype(BF16)
        vh = kv[:, width + h * MEM_HD:width + (h + 1) * MEM_HD].astype(BF16)
        p = _softmax_plain(_dot_nt(q[:, sl].astype(BF16), kh) * (MEM_HD ** -0.5))
        o_ref[:, sl] = _dot(p.astype(BF16), vh)


def _xattn_prompt_kernel(q_ref, kv_ref, o_ref):
    _xattn_heads(q_ref[...], kv_ref[...], o_ref)


def _xattn_sample_kernel(q_ref, *rest):
    nc = MEM_HD // LANE
    k_refs, v_refs, o_ref = rest[:nc], rest[nc:2 * nc], rest[-1]
    q = q_ref[...]
    tq = q.shape[0]
    n_mem = k_refs[0].shape[0]
    rows, cols = MEM_HEADS * tq, MEM_HEADS * n_mem
    s = None
    for c in range(nc):
        qc = jnp.concatenate([q[:, h * MEM_HD + c * LANE:h * MEM_HD + (c + 1) * LANE]
                              for h in range(MEM_HEADS)], axis=0).astype(BF16)
        part = _dot_nt(qc, k_refs[c][...].reshape(cols, LANE).astype(BF16))
        s = part if s is None else s + part
    row_head = lax.broadcasted_iota(I32, (rows, 1), 0) // tq
    col_head = lax.broadcasted_iota(I32, (1, cols), 1) % MEM_HEADS
    p = _softmax_rows(s * (MEM_HD ** -0.5), row_head == col_head).astype(BF16)
    for c in range(nc):
        oc = _dot(p, v_refs[c][...].reshape(cols, LANE).astype(BF16))
        for h in range(MEM_HEADS):
            o_ref[:, h * MEM_HD + c * LANE:h * MEM_HD + (c + 1) * LANE] = oc[h * tq:(h + 1) * tq]


def cross_attention(qm, mp, kv_prompt, kv_cache, layer, *, ts, tm=512):
    m, width = qm.shape
    bs = (m - mp) // ts
    assert mp % tm == 0 and mp % ts == 0
    o = pl.pallas_call(
        _xattn_prompt_kernel,
        grid=(mp // tm,),
        in_specs=[pl.BlockSpec((tm, width), lambda i: (i, 0)),
                  pl.BlockSpec(kv_prompt.shape, lambda i: (0, 0))],
        out_specs=pl.BlockSpec((tm, width), lambda i: (i, 0)),
        out_shape=jax.ShapeDtypeStruct((m, width), F32),
        compiler_params=_params(("parallel",)),
        name="xattn_prompt",
    )(qm, kv_prompt)
    rb = mp // ts
    nc = MEM_HD // LANE
    slab = lambda kv, c: pl.BlockSpec((None, None, N_MEM, None, MEM_HEADS, LANE),
                                      lambda b: (layer, b, 0, kv, 0, c))
    slabs = [slab(kv, c) for kv in range(2) for c in range(nc)]
    return pl.pallas_call(
        _xattn_sample_kernel,
        grid=(bs,),
        in_specs=[pl.BlockSpec((ts, width), lambda b: (rb + b, 0))] + slabs
                 + [pl.BlockSpec(memory_space=pl.ANY)],
        out_specs=pl.BlockSpec((ts, width), lambda b: (rb + b, 0)),
        out_shape=jax.ShapeDtypeStruct((m, width), F32),
        input_output_aliases={1 + len(slabs): 0},
        compiler_params=_params(("parallel",)),
        name="xattn_sample",
    )(qm, *([kv_cache] * len(slabs)), o)


def _gate_weight(w_in):
    per = N_GATE // NSA_HKV
    parts = [jnp.pad(w_in[:, N_MAIN0 + h * per:N_MAIN0 + (h + 1) * per], ((0, 0), (0, LANE - per)))
             for h in range(NSA_HKV)]
    return jnp.concatenate(parts, axis=1)


def kernel(x_prompt, x_sample, cache_nsa_kv, state_nsa_win_kv, state_lru_conv, state_lru_h, state_sconv, cache_mem_kv, page_table, mem_prompt, w_in0, lru_conv_w, lru_conv_b, lru_wx, lru_bx, lru_wa, lru_ba, lru_lambda, w_out0, w_in1, sconv_w, w_out1, w_q_mem, w_kv_mem, w_o_mem, w_gu, w_down, ln_g, ln_b):
    bp, tp = x_prompt.shape[:2]
    bs, ts = x_sample.shape[:2]
    assert bp == 1
    mp, ms = bp * tp, bs * ts
    past_len = page_table.shape[1] * PAGE_SIZE
    pos_all = jnp.concatenate([jnp.arange(tp, dtype=I32),
                               jnp.tile(past_len + jnp.arange(ts, dtype=I32), bs)])
    rope_cos, rope_sin = _rope_tables(pos_all)
    xa = jnp.concatenate([x_prompt.reshape(mp, D_MODEL), x_sample.reshape(ms, D_MODEL)], axis=0)
    xa_bf = xa.astype(BF16)
    p_nsa, p_win, p_conv, p_h, p_sc, p_mem = [], [], [], [], [], []
    s_nsa, s_win, s_conv, s_h, s_sc = [], [], [], [], []
    for l in range(DEPTH):
        if l % 2 == 0:
            e = l // 2
            lp = (lru_conv_w[e], lru_conv_b[e], lru_wx[e], lru_bx[e], lru_wa[e], lru_ba[e], lru_lambda[e])
            proj = matmul(xa_bf, w_in0[e].astype(BF16), n_cols=N_MAIN0)
            gt = matmul(xa_bf, _gate_weight(w_in0[e]).astype(BF16))
            q_bf, kv4, kv_bf, kvw, kw_bf, gates, ch = nsa_prep(proj, gt, rope_cos, rope_sin)
            y_lru, h_p, h_s = rglru(proj, mp, state_lru_conv[e], state_lru_h[e], *lp, bs=bs, ts=ts)
            nq, nkv, nkw = NSA_HQ * NSA_HD, 4 * NSA_KVW, 2 * NSA_KVW
            n_pool = cache_nsa_kv.shape[1]
            o_nsa = nsa_prompt(q_bf, gates, ch, kv_bf, kw_bf, tp)
            o_nsa, win_s = nsa_sample(q_bf[mp:].reshape(bs, ts, nq), gates, kv4, kvw, mp,
                                      cache_nsa_kv.reshape(-1, NSA_HD), e * n_pool, page_table,
                                      state_nsa_win_kv.reshape(-1, NSA_HD), e * bs, state_nsa_win_kv.shape[2],
                                      o_nsa)
            mix = jnp.concatenate([y_lru, o_nsa], axis=1).astype(BF16)
            xb_s = proj[mp:, :W_LRU].reshape(bs, ts, W_LRU)
            n_kv, n_kw = nkv // NSA_HD, nkw // NSA_HD
            p_nsa.append(kv4[:mp * n_kv].reshape(bp, tp, 4, NSA_HKV, NSA_HD))
            p_win.append(kvw[(mp - min(WINDOW, tp)) * n_kw:mp * n_kw].reshape(bp, -1, 2, NSA_HKV, NSA_HD))
            p_conv.append(proj[mp - (LRU_CONV - 1):mp, :W_LRU].reshape(bp, LRU_CONV - 1, W_LRU))
            p_h.append(h_p)
            s_nsa.append(kv4[mp * n_kv:].reshape(bs, ts, 4, NSA_HKV, NSA_HD))
            s_win.append(win_s.reshape(state_nsa_win_kv.shape[1:]))
            s_conv.append(xb_s[:, ts - (LRU_CONV - 1):])
            s_h.append(h_s.reshape(bs, ts, W_LRU)[:, ts - 1])
            w_out = w_out0[e]
        else:
            o = l // 2
            h3 = matmul(xa_bf, w_in1[o].astype(BF16))
            mix, z = short_conv(h3, mp, state_sconv[o], sconv_w[o], ts=ts)
            p_sc.append(z[mp - (SC_CONV - 1):mp].reshape(bp, SC_CONV - 1, D_MODEL))
            s_sc.append(z[mp:].reshape(bs, ts, D_MODEL)[:, ts - (SC_CONV - 1):])
            w_out = w_out1[o]
        xa, xa_bf = matmul_postnorm(mix, w_out.astype(BF16), xa, ln_g[l, 0], ln_b[l, 0])
        kv_mem_p = matmul(mem_prompt.reshape(bp * N_MEM, D_MODEL).astype(BF16), w_kv_mem[l].astype(BF16))
        p_mem.append(kv_mem_p.reshape(bp, N_MEM, 2, MEM_HEADS, MEM_HD))
        qm = matmul(xa_bf, w_q_mem[l].astype(BF16))
        om = cross_attention(qm, mp, kv_mem_p, cache_mem_kv, l, ts=ts)
        xa, xa_bf = matmul_postnorm(om.astype(BF16), w_o_mem[l].astype(BF16), xa, ln_g[l, 1], ln_b[l, 1])
        hf = matmul_swiglu(xa_bf, w_gu[l].astype(BF16))
        xa, xa_bf = matmul_postnorm(hf, w_down[l].astype(BF16), xa, ln_g[l, 2], ln_b[l, 2])
    return (xa[:mp].reshape(bp, tp, D_MODEL), xa[mp:].reshape(bs, ts, D_MODEL),
            jnp.stack(p_nsa), jnp.stack(p_win), jnp.stack(p_conv), jnp.stack(p_h), jnp.stack(p_sc),
            jnp.stack(p_mem), jnp.stack(s_nsa), jnp.stack(s_win), jnp.stack(s_conv), jnp.stack(s_h),
            jnp.stack(s_sc))
```

```python
import functools

import numpy as np
import jax
import jax.numpy as jnp
from jax import lax
from jax.experimental import pallas as pl
from jax.experimental.pallas import tpu as pltpu

D_MODEL = 2048
SEQ = 8192
DEPTH = 2
DEC_BATCH = 128
DEC_SEQ = 8
PAGE_SIZE = 128

ALPHA = (2.0 * DEPTH) ** 0.25
LN_EPS = 1e-5
W_LRU = D_MODEL // 2
LRU_CONV = 4
LRU_BLOCKS = 8
LRU_BLOCK = W_LRU // LRU_BLOCKS
LRU_C = 8.0
NSA_HQ = 8
NSA_HKV = 2
NSA_HD = 128
NSA_G = NSA_HQ // NSA_HKV
NSA_KVW = NSA_HKV * NSA_HD
CMP_STRIDE = 16
CMP_LEN = 2 * CMP_STRIDE
SEL_BLOCK = 64
SEL_TOPK = 16
WINDOW = 512
Q_BLOCK = 128
ROPE_THETA = 10000.0
BIG = 1e6
NEG = -1e30
NSA_SCALE = NSA_HD ** -0.5
EVEN_SPLITS = [W_LRU, 2 * W_LRU, 2 * W_LRU + NSA_HQ * NSA_HD,
               2 * W_LRU + NSA_HQ * NSA_HD + 4 * NSA_KVW,
               2 * W_LRU + NSA_HQ * NSA_HD + 6 * NSA_KVW]
N_MAIN0 = EVEN_SPLITS[-1]
N_GATE = 3 * NSA_HQ
SC_CONV = 3
N_MEM = 256
MEM_HEADS = 4
MEM_HD = D_MODEL // MEM_HEADS
D_FF = ((8 * D_MODEL + 3 * 256 - 1) // (3 * 256)) * 256

LANE = 128
VMEM_LIMIT = 56 * 1024 * 1024
SEL_TILE = 512
MM_TILE_M = 1024
MM_TILE_N = 1024
MM_MAX_K = 2048
LN_TILE_M = 512
LN_K_SPLIT = 4

F32 = jnp.float32
BF16 = jnp.bfloat16
I32 = jnp.int32


def _params(sem):
    return pltpu.CompilerParams(dimension_semantics=sem, vmem_limit_bytes=VMEM_LIMIT)


def _dot(a, b):
    return jnp.dot(a, b, preferred_element_type=F32)


def _as_bf16(x):
    return x if x.dtype == BF16 else x.astype(BF16)


def _dot_nt(a, b):
    return lax.dot_general(a, b, (((1,), (1,)), ((), ())), preferred_element_type=F32)


def _mm_kernel(x_ref, w_ref, o_ref):
    o_ref[...] = _dot(x_ref[...], _as_bf16(w_ref[...])).astype(o_ref.dtype)


def _pick_tile(n, prefs):
    for t in prefs:
        if n % t == 0:
            return t
    return n


def matmul(x, w, *, n_cols=None, out_dtype=F32):
    m, kdim = x.shape
    n = w.shape[1] if n_cols is None else n_cols
    assert x.dtype == BF16
    tm = _pick_tile(m, (MM_TILE_M,))
    tn = _pick_tile(n, (MM_TILE_N, MM_TILE_N // 2, MM_TILE_N // 4))
    return pl.pallas_call(
        _mm_kernel,
        grid=(m // tm, n // tn),
        in_specs=[pl.BlockSpec((tm, kdim), lambda i, j: (i, 0)),
                  pl.BlockSpec((kdim, tn), lambda i, j: (0, j))],
        out_specs=pl.BlockSpec((tm, tn), lambda i, j: (i, j)),
        out_shape=jax.ShapeDtypeStruct((m, n), out_dtype),
        compiler_params=_params(("parallel", "parallel")),
        name="matmul",
    )(x, w)


def _mm_ln_kernel(x_ref, w_ref, res_ref, g_ref, b_ref, o_ref, obf_ref, *acc, nk):
    part = _dot(x_ref[...], w_ref[...])

    def finish(y):
        z = ALPHA * res_ref[...] + y
        mu = jnp.mean(z, axis=-1, keepdims=True)
        zc = z - mu
        var = jnp.mean(zc * zc, axis=-1, keepdims=True)
        out = zc * lax.rsqrt(var + LN_EPS) * g_ref[...] + b_ref[...]
        o_ref[...] = out
        obf_ref[...] = out.astype(BF16)

    if nk == 1:
        finish(part)
        return
    acc_ref, = acc
    k = pl.program_id(1)

    @pl.when(k == 0)
    def _():
        acc_ref[...] = part

    @pl.when(k > 0)
    def _():
        acc_ref[...] += part

    @pl.when(k == nk - 1)
    def _():
        finish(acc_ref[...])


def matmul_postnorm(x, w, res, g, b):
    m, kdim = x.shape
    n = w.shape[1]
    assert x.dtype == BF16 and w.dtype == BF16 and res.shape == (m, n)
    tm = _pick_tile(m, (LN_TILE_M,))
    nk = 1 if kdim <= MM_MAX_K else LN_K_SPLIT
    assert kdim % (nk * LANE) == 0
    tk = kdim // nk
    return pl.pallas_call(
        functools.partial(_mm_ln_kernel, nk=nk),
        grid=(m // tm, nk),
        in_specs=[pl.BlockSpec((tm, tk), lambda i, k: (i, k)),
                  pl.BlockSpec((tk, n), lambda i, k: (k, 0)),
                  pl.BlockSpec((tm, n), lambda i, k: (i, 0)),
                  pl.BlockSpec((1, n), lambda i, k: (0, 0)),
                  pl.BlockSpec((1, n), lambda i, k: (0, 0))],
        out_specs=[pl.BlockSpec((tm, n), lambda i, k: (i, 0)), pl.BlockSpec((tm, n), lambda i, k: (i, 0))],
        out_shape=[jax.ShapeDtypeStruct((m, n), F32), jax.ShapeDtypeStruct((m, n), BF16)],
        scratch_shapes=[] if nk == 1 else [pltpu.VMEM((tm, n), F32)],
        compiler_params=_params(("parallel", "arbitrary")),
        name="matmul_postnorm",
    )(x, w, res, g.reshape(1, n), b.reshape(1, n))


def _mm_swiglu_kernel(x_ref, wg_ref, wu_ref, o_ref):
    x = x_ref[...]
    g = _dot(x, _as_bf16(wg_ref[...]))
    u = _dot(x, _as_bf16(wu_ref[...]))
    o_ref[...] = (g * jax.nn.sigmoid(g) * u).astype(o_ref.dtype)


def matmul_swiglu(x, w_gu):
    m, kdim = x.shape
    f = w_gu.shape[1] // 2
    assert x.dtype == BF16
    tm = _pick_tile(m, (MM_TILE_M,))
    tn = _pick_tile(f, (MM_TILE_N // 2, MM_TILE_N // 4))
    nj = f // tn
    return pl.pallas_call(
        _mm_swiglu_kernel,
        grid=(m // tm, nj),
        in_specs=[pl.BlockSpec((tm, kdim), lambda i, j: (i, 0)),
                  pl.BlockSpec((kdim, tn), lambda i, j: (0, j)),
                  pl.BlockSpec((kdim, tn), lambda i, j: (0, j + nj))],
        out_specs=pl.BlockSpec((tm, tn), lambda i, j: (i, j)),
        out_shape=jax.ShapeDtypeStruct((m, f), BF16),
        compiler_params=_params(("parallel", "parallel")),
        name="matmul_swiglu",
    )(x, w_gu, w_gu)


def _rope_tables(pos):
    half = NSA_HD // 2
    inv = ROPE_THETA ** (-jnp.arange(half, dtype=F32) / half)
    ang = pos.astype(F32)[:, None] * inv[None, :]
    cos, sin = jnp.cos(ang), jnp.sin(ang)
    return jnp.concatenate([cos, cos], axis=-1), jnp.concatenate([-sin, sin], axis=-1)


def _nsa_prep_kernel(q_ref, kv_ref, kw_ref, gt_ref, cos_ref, sin_ref,
                     qbf_ref, kv_o_ref, kvbf_ref, kw_o_ref, kwbf_ref, g_o_ref, ch_ref):
    cos = cos_ref[...]
    sin = sin_ref[...]
    rows = q_ref.shape[0]

    def rot(x):
        return x * cos + pltpu.roll(x, NSA_HD // 2, axis=1) * sin

    for h in range(NSA_HQ):
        sl = slice(NSA_HD * h, NSA_HD * (h + 1))
        qbf_ref[:, sl] = rot(q_ref[:, sl]).astype(BF16)
    for c in range(4 * NSA_HKV):
        sl = slice(NSA_HD * c, NSA_HD * (c + 1))
        slot = c // NSA_HKV
        x = kv_ref[:, sl]
        if slot in (0, 2):
            x = rot(x)
        kv_o_ref[pl.ds(c, rows, stride=4 * NSA_HKV), :] = x
        kvbf_ref[:, sl] = x.astype(BF16)
        if slot < 2:
            ch_ref[:, sl] = jnp.sum(x.reshape(rows // CMP_STRIDE, CMP_STRIDE, NSA_HD), axis=1)
    for c in range(2 * NSA_HKV):
        sl = slice(NSA_HD * c, NSA_HD * (c + 1))
        x = kw_ref[:, sl]
        if c // NSA_HKV == 0:
            x = rot(x)
        kw_o_ref[pl.ds(c, rows, stride=2 * NSA_HKV), :] = x
        kwbf_ref[:, sl] = x.astype(BF16)
    g_o_ref[...] = jax.nn.sigmoid(gt_ref[...])


def nsa_prep(proj, gt, cos, sin, *, tm=512):
    m = proj.shape[0]
    assert m % tm == 0
    qc = EVEN_SPLITS[1] // (NSA_HQ * NSA_HD)
    kc = EVEN_SPLITS[2] // (4 * NSA_KVW)
    wc = EVEN_SPLITS[3] // (2 * NSA_KVW)
    nq, nkv, nkw, ng = NSA_HQ * NSA_HD, 4 * NSA_KVW, 2 * NSA_KVW, NSA_HKV * LANE
    row = lambda n: pl.BlockSpec((tm, n), lambda i: (i, 0))
    return pl.pallas_call(
        _nsa_prep_kernel,
        grid=(m // tm,),
        in_specs=[pl.BlockSpec((tm, nq), lambda i: (i, qc)),
                  pl.BlockSpec((tm, nkv), lambda i: (i, kc)),
                  pl.BlockSpec((tm, nkw), lambda i: (i, wc)),
                  row(ng), row(NSA_HD), row(NSA_HD)],
        out_specs=[row(nq), pl.BlockSpec((tm * nkv // NSA_HD, NSA_HD), lambda i: (i, 0)), row(nkv),
                   pl.BlockSpec((tm * nkw // NSA_HD, NSA_HD), lambda i: (i, 0)), row(nkw), row(ng),
                   pl.BlockSpec((tm // CMP_STRIDE, 2 * NSA_KVW), lambda i: (i, 0))],
        out_shape=[jax.ShapeDtypeStruct((m, nq), BF16),
                   jax.ShapeDtypeStruct((m * nkv // NSA_HD, NSA_HD), F32),
                   jax.ShapeDtypeStruct((m, nkv), BF16),
                   jax.ShapeDtypeStruct((m * nkw // NSA_HD, NSA_HD), F32),
                   jax.ShapeDtypeStruct((m, nkw), BF16),
                   jax.ShapeDtypeStruct((m, ng), F32),
                   jax.ShapeDtypeStruct((m // CMP_STRIDE, 2 * NSA_KVW), F32)],
        compiler_params=_params(("parallel",)),
        name="nsa_prep",
    )(proj, proj, proj, gt, cos, sin)


def _softmax_rows(s, mask):
    sm = jnp.where(mask, s, NEG)
    e = jnp.where(mask, jnp.exp(sm - jnp.max(sm, axis=-1, keepdims=True)), 0.0)
    return e / jnp.maximum(jnp.sum(e, axis=-1, keepdims=True), 1e-30)


def _block_importance(imp, pmat):
    hi = imp.astype(BF16)
    r1 = imp - hi.astype(F32)
    mid = r1.astype(BF16)
    lo = (r1 - mid.astype(F32)).astype(BF16)
    return _dot(hi, pmat) + _dot(mid, pmat) + _dot(lo, pmat)


def _select_blocks(imp_sel, posq):
    blk = lax.broadcasted_iota(I32, (1, LANE), 1)
    cur = posq // SEL_BLOCK
    forced = (blk == 0) | (blk == cur) | (blk == cur - 1)
    valid = blk * SEL_BLOCK <= posq
    score = jnp.where(valid, jnp.where(forced, BIG, imp_sel), -BIG)
    key = lax.bitcast_convert_type(score, I32)
    key = jnp.where(key < 0, key ^ 0x7FFFFFFF, key)
    key_m1 = key - 1
    cnt = jnp.zeros(score.shape, I32)
    for i in range(LANE):
        thr = jnp.where(blk > i, key_m1, key)
        cnt = cnt + (key[:, i:i + 1] > thr).astype(I32)
    return (cnt < SEL_TOPK).astype(F32)


def _pool_matrix(n_cmp):
    c = np.arange(n_cmp)[:, None]
    j = np.arange(LANE)[None, :]
    return jnp.asarray(((c // 4 == j).astype(np.float32) + ((c + 1) // 4 == j)), dtype=BF16)


def _select_blocks_wide(imp_sel, posq, n_live, key_ref):
    blk = lax.broadcasted_iota(I32, (1, LANE), 1)
    cur = posq // SEL_BLOCK
    forced = (blk == 0) | (blk == cur) | (blk == cur - 1)
    valid = blk * SEL_BLOCK <= posq
    score_t = jnp.where(valid, jnp.where(forced, BIG, imp_sel), -BIG).T
    key = lax.bitcast_convert_type(score_t, I32)
    key = jnp.where(key < 0, key ^ 0x7FFFFFFF, key)
    key_m1 = key - 1
    key_ref[...] = key
    blk_s = lax.broadcasted_iota(I32, (LANE, 1), 0)

    def body(c, cnt):
        base = pl.multiple_of(c * 8, 8)
        cand = key_ref[pl.ds(base, 8), :]
        for k in range(8):
            thr = jnp.where(blk_s > base + k, key_m1, key)
            cnt = cnt + (cand[k:k + 1, :] > thr).astype(I32)
        return cnt

    cnt = lax.fori_loop(0, n_live // 8, body, jnp.zeros((LANE, LANE), I32))
    return (cnt < SEL_TOPK).astype(F32).T


def _nsa_prompt_kernel(*refs, n_qb):
    o_ref = refs[9]
    qb = pl.program_id(1)

    @pl.when(qb < n_qb)
    def _():
        _nsa_prompt_step(*refs)

    @pl.when(qb >= n_qb)
    def _():
        o_ref[...] = jnp.zeros(o_ref.shape, o_ref.dtype)


def _nsa_prompt_step(q_ref, g_ref, chk_ref, chv_ref, ks_ref, vs_ref, kw_ref, vw_ref, pmat_ref,
                     o_ref, kct_ref, vc_ref, key_ref):
    qb = pl.program_id(1)
    n_cmp = chk_ref.shape[0]

    @pl.when(qb == 0)
    def _():
        ck = chk_ref[...]
        kc = (ck + pltpu.roll(ck, n_cmp - 1, axis=0)) * (1.0 / CMP_LEN)
        kct_ref[...] = kc.T.astype(BF16)
        cv = chv_ref[...]
        vc_ref[...] = ((cv + pltpu.roll(cv, n_cmp - 1, axis=0)) * (1.0 / CMP_LEN)).astype(BF16)

    qblk = q_ref[...]
    q2 = jnp.concatenate([qblk[:, NSA_HD * g:NSA_HD * (g + 1)] for g in range(NSA_G)], axis=0)
    rows = NSA_G * Q_BLOCK
    posq = qb * Q_BLOCK + lax.broadcasted_iota(I32, (Q_BLOCK, 1), 0)
    posq4 = jnp.concatenate([posq] * NSA_G, axis=0)

    s = _dot(q2, kct_ref[...]) * NSA_SCALE
    cend = lax.broadcasted_iota(I32, (1, n_cmp), 1) * CMP_STRIDE + (CMP_LEN - 1)
    p = _softmax_rows(s, cend <= posq4)
    o_cmp = _dot(p.astype(BF16), vc_ref[...])
    imp = p[0:Q_BLOCK]
    for g in range(1, NSA_G):
        imp = imp + p[g * Q_BLOCK:(g + 1) * Q_BLOCK]
    n_live = ((qb + 1) * (Q_BLOCK // SEL_BLOCK) + 7) // 8 * 8
    sel = _select_blocks_wide(_block_importance(imp, pmat_ref[...]), posq, n_live, key_ref).astype(BF16)

    brow = lax.broadcasted_iota(I32, (LANE, 1), 0)
    kcol = lax.broadcasted_iota(I32, (1, SEL_TILE), 1)
    kdiv = kcol // SEL_BLOCK

    def tile_update(t, carry):
        m, l, acc = carry
        off = pl.multiple_of(t * SEL_TILE, SEL_TILE)
        k = ks_ref[pl.ds(off, SEL_TILE), :]
        v = vs_ref[pl.ds(off, SEL_TILE), :]
        st = _dot_nt(q2, k) * NSA_SCALE
        expand = (brow == kdiv + t * (SEL_TILE // SEL_BLOCK)).astype(BF16)
        chosen = _dot(sel, expand)
        msk = (chosen > 0.5) & (kcol + t * SEL_TILE <= posq)
        sm = st.reshape(NSA_G, Q_BLOCK, SEL_TILE) + jnp.where(msk, 0.0, NEG)[None]
        m_new = jnp.maximum(m, jnp.max(sm, axis=-1, keepdims=True))
        alpha = jnp.exp(m - m_new)
        e = jnp.exp(sm - m_new)
        l = alpha * l + jnp.sum(e, axis=-1, keepdims=True)
        pv = _dot(e.reshape(rows, SEL_TILE).astype(BF16), v)
        acc = alpha.reshape(rows, 1) * acc + pv
        return m_new, l, acc

    def body(t2, carry):
        return tile_update(2 * t2 + 1, tile_update(2 * t2, carry))

    m0 = jnp.full((NSA_G, Q_BLOCK, 1), NEG, F32)
    l0 = jnp.zeros((NSA_G, Q_BLOCK, 1), F32)
    n_pairs = qb // (2 * SEL_TILE // Q_BLOCK) + 1
    _, l, acc = lax.fori_loop(0, n_pairs, body, (m0, l0, jnp.zeros((rows, NSA_HD), F32)))
    o_sel = acc / jnp.maximum(l.reshape(rows, 1), 1e-30)

    span = WINDOW + Q_BLOCK
    start = pl.multiple_of(jnp.maximum(qb - WINDOW // Q_BLOCK, 0) * Q_BLOCK, Q_BLOCK)
    kw = kw_ref[pl.ds(start, span), :]
    vw = vw_ref[pl.ds(start, span), :]
    s = _dot_nt(q2, kw) * NSA_SCALE
    dpos = posq4 - (start + lax.broadcasted_iota(I32, (1, span), 1))
    p = _softmax_rows(s, (dpos >= 0) & (dpos < WINDOW))
    o_win = _dot(p.astype(BF16), vw)

    gates = g_ref[...]
    for g in range(NSA_G):
        rs = slice(g * Q_BLOCK, (g + 1) * Q_BLOCK)
        o_ref[:, NSA_HD * g:NSA_HD * (g + 1)] = (gates[:, 3 * g:3 * g + 1] * o_cmp[rs]
                                                + gates[:, 3 * g + 1:3 * g + 2] * o_sel[rs]
                                                + gates[:, 3 * g + 2:3 * g + 3] * o_win[rs])


def nsa_prompt(q_bf, gates, ch, kv_bf, kw_bf, t):
    m = q_bf.shape[0]
    n_qb = t // Q_BLOCK
    n_cmp = t // CMP_STRIDE
    assert m % Q_BLOCK == 0 and t % (2 * SEL_TILE) == 0
    gw = NSA_G * NSA_HD
    res = lambda c0: pl.BlockSpec((t, NSA_HD), lambda h, i: (0, c0 + h))
    return pl.pallas_call(
        functools.partial(_nsa_prompt_kernel, n_qb=n_qb),
        grid=(NSA_HKV, m // Q_BLOCK),
        in_specs=[pl.BlockSpec((Q_BLOCK, gw), lambda h, i: (jnp.minimum(i, n_qb - 1), h)),
                  pl.BlockSpec((Q_BLOCK, LANE), lambda h, i: (jnp.minimum(i, n_qb - 1), h)),
                  pl.BlockSpec((n_cmp, NSA_HD), lambda h, i: (0, h)),
                  pl.BlockSpec((n_cmp, NSA_HD), lambda h, i: (0, NSA_HKV + h)),
                  res(2 * NSA_HKV), res(3 * NSA_HKV), res(0), res(NSA_HKV),
                  pl.BlockSpec((n_cmp, LANE), lambda h, i: (0, 0))],
        out_specs=pl.BlockSpec((Q_BLOCK, gw), lambda h, i: (i, h)),
        out_shape=jax.ShapeDtypeStruct((m, NSA_HQ * NSA_HD), F32),
        scratch_shapes=[pltpu.VMEM((NSA_HD, n_cmp), BF16), pltpu.VMEM((n_cmp, NSA_HD), BF16),
                        pltpu.VMEM((LANE, LANE), I32)],
        compiler_params=_params(("parallel", "arbitrary")),
        name="nsa_prompt",
    )(q_bf, gates, ch, ch, kv_bf, kv_bf, kw_bf, kw_bf, _pool_matrix(n_cmp))


def _nsa_sample_kernel(pt_ref, q_ref, g_ref, kvn_ref, kwn_ref, win_ref, *rest, n_pages, past_len):
    pages = rest[:n_pages]
    pmat_ref, emat_ref, _, o_ref, nwin_ref = rest[n_pages:]
    tq = q_ref.shape[1]
    n_kv, n_kw = 4 * NSA_HKV, 2 * NSA_HKV
    w_buf = win_ref.shape[0] // n_kw
    pad_rows = PAGE_SIZE - tq
    qf = q_ref[0].astype(F32)
    gates = g_ref[...]

    def rows_of(ref, c, n, per):
        return ref[pl.ds(c, n, stride=per), :]
    posq = past_len + lax.broadcasted_iota(I32, (tq, 1), 0)
    posq4 = jnp.concatenate([posq] * NSA_G, axis=0)
    zpad = jnp.zeros((pad_rows, NSA_HD), F32)

    def new_page(x):
        return jnp.concatenate([x, zpad], axis=0).astype(BF16)

    def chunk_sums(pg, c):
        x = rows_of(pg, c, PAGE_SIZE, n_kv)
        return jnp.sum(x.reshape(PAGE_SIZE // CMP_STRIDE, CMP_STRIDE, NSA_HD), axis=1)

    n_cmp = past_len // CMP_STRIDE
    crow = lax.broadcasted_iota(I32, (n_cmp, 1), 0)

    def compressed(c):
        ch = jnp.concatenate([chunk_sums(pg, c) for pg in pages], axis=0)
        ch_new = jnp.sum(rows_of(kvn_ref, c, tq, n_kv), axis=0, keepdims=True)
        nxt = jnp.where(crow == n_cmp - 1, ch_new, pltpu.roll(ch, n_cmp - 1, axis=0))
        return ((ch + nxt) * (1.0 / CMP_LEN)).astype(BF16)

    for h in range(NSA_HKV):
        q2 = jnp.concatenate([qf[:, NSA_HD * (NSA_G * h + g):NSA_HD * (NSA_G * h + g + 1)]
                              for g in range(NSA_G)], axis=0).astype(BF16)
        col = lambda slot: slot * NSA_HKV + h

        kc = compressed(col(0))
        vc = compressed(col(1))
        s = _dot_nt(q2, kc) * NSA_SCALE
        cend = lax.broadcasted_iota(I32, (1, n_cmp), 1) * CMP_STRIDE + (CMP_LEN - 1)
        p = _softmax_rows(s, cend <= posq4)
        o_cmp = _dot(p.astype(BF16), vc)
        imp = p[0:tq]
        for g in range(1, NSA_G):
            imp = imp + p[g * tq:(g + 1) * tq]
        sel = _select_blocks(_block_importance(imp, pmat_ref[...]), posq)
        sel4 = jnp.concatenate([sel] * NSA_G, axis=0).astype(BF16)

        chosen = _dot(sel4, emat_ref[...])
        pieces = [_dot_nt(q2, rows_of(pg, col(2), PAGE_SIZE, n_kv).astype(BF16)) for pg in pages]
        pieces.append(_dot_nt(q2, new_page(rows_of(kvn_ref, col(2), tq, n_kv))))
        s = jnp.concatenate(pieces, axis=1) * NSA_SCALE
        n_key = s.shape[1]
        kpos = lax.broadcasted_iota(I32, (1, n_key), 1)
        p = _softmax_rows(s, (chosen > 0.5) & (kpos <= posq4)).astype(BF16)
        o_sel = _dot(p[:, past_len:], new_page(rows_of(kvn_ref, col(3), tq, n_kv)))
        for i, pg in enumerate(pages):
            o_sel = o_sel + _dot(p[:, i * PAGE_SIZE:(i + 1) * PAGE_SIZE],
                                 rows_of(pg, col(3), PAGE_SIZE, n_kv).astype(BF16))

        kc_w, vc_w = h, NSA_HKV + h
        s = jnp.concatenate([_dot_nt(q2, rows_of(win_ref, kc_w, w_buf, n_kw).astype(BF16)),
                             _dot_nt(q2, new_page(rows_of(kwn_ref, kc_w, tq, n_kw)))], axis=1) * NSA_SCALE
        posw = (past_len - w_buf) + lax.broadcasted_iota(I32, (1, w_buf + PAGE_SIZE), 1)
        dpos = posq4 - posw
        p = _softmax_rows(s, (dpos >= 0) & (dpos < WINDOW) & (posw >= 0)).astype(BF16)
        o_win = (_dot(p[:, :w_buf], rows_of(win_ref, vc_w, w_buf, n_kw).astype(BF16))
                 + _dot(p[:, w_buf:], new_page(rows_of(kwn_ref, vc_w, tq, n_kw))))

        for g in range(NSA_G):
            rs = slice(g * tq, (g + 1) * tq)
            gc = h * LANE + 3 * g
            head = NSA_G * h + g
            o_ref[:, NSA_HD * head:NSA_HD * (head + 1)] = (
                gates[:, gc:gc + 1] * o_cmp[rs] + gates[:, gc + 1:gc + 2] * o_sel[rs]
                + gates[:, gc + 2:gc + 3] * o_win[rs])

    nwin_ref[0:(w_buf - tq) * n_kw, :] = win_ref[tq * n_kw:w_buf * n_kw, :]
    nwin_ref[(w_buf - tq) * n_kw:w_buf * n_kw, :] = kwn_ref[...]


def nsa_sample(q_bf, gates, kv_new, kw_new, mp, pool, page_base, page_table, win_all, win_base, w_buf, o_buf):
    bsz, tq, nq = q_bf.shape
    n_pages = page_table.shape[1]
    past_len = n_pages * PAGE_SIZE
    n_kv, n_kw = 4 * NSA_HKV, 2 * NSA_HKV
    assert tq == 8 and past_len % SEL_BLOCK == 0 and mp % tq == 0
    n_cmp = past_len // CMP_STRIDE
    n_key = past_len + PAGE_SIZE
    rb = mp // tq
    emat = jnp.asarray(np.arange(LANE)[:, None] == (np.arange(n_key)[None, :] // SEL_BLOCK), dtype=BF16)
    seq = lambda r, n: pl.BlockSpec((r, n), lambda b, pt: (rb + b, 0))
    page = lambda p: pl.BlockSpec((PAGE_SIZE * n_kv, NSA_HD),
                                  lambda b, pt: (page_base + pt[b * n_pages + p], 0))
    const = lambda shape: pl.BlockSpec(shape, lambda b, pt: (0, 0))
    return pl.pallas_call(
        functools.partial(_nsa_sample_kernel, n_pages=n_pages, past_len=past_len),
        grid_spec=pltpu.PrefetchScalarGridSpec(
            num_scalar_prefetch=1,
            grid=(bsz,),
            in_specs=[pl.BlockSpec((1, tq, nq), lambda b, pt: (b, 0, 0)),
                      seq(tq, gates.shape[1]), seq(tq * n_kv, NSA_HD), seq(tq * n_kw, NSA_HD),
                      pl.BlockSpec((w_buf * n_kw, NSA_HD), lambda b, pt: (win_base + b, 0))]
                     + [page(p) for p in range(n_pages)]
                     + [const((n_cmp, LANE)), const((LANE, n_key)), pl.BlockSpec(memory_space=pl.ANY)],
            out_specs=[seq(tq, nq), pl.BlockSpec((w_buf * n_kw, NSA_HD), lambda b, pt: (b, 0))]),
        out_shape=[jax.ShapeDtypeStruct(o_buf.shape, F32),
                   jax.ShapeDtypeStruct((bsz * w_buf * n_kw, NSA_HD), F32)],
        input_output_aliases={8 + n_pages: 0},
        compiler_params=_params(("parallel",)),
        name="nsa_sample",
    )(page_table.reshape(-1), q_bf, gates, kv_new, kw_new, win_all, *([pool] * n_pages),
      _pool_matrix(n_cmp), emat, o_buf)


def _shift_rows(x, s):
    return pltpu.roll(x, s % x.shape[0], axis=0)


def _lru_recurrence_inputs(xc, wx_ref, bx_ref, wa_ref, ba_ref, lam_ref):
    xb = xc.astype(BF16)
    i_g = jax.nn.sigmoid(_dot(xb, wx_ref[0].astype(BF16)) + bx_ref[...])
    r_g = jax.nn.sigmoid(_dot(xb, wa_ref[0].astype(BF16)) + ba_ref[...])
    z = -lam_ref[...]
    softplus = jnp.maximum(z, 0.0) + jnp.log1p(jnp.exp(-jnp.abs(z)))
    log_a = -LRU_C * r_g * softplus
    a = jnp.exp(log_a)
    th = jnp.tanh(log_a)
    u = jnp.sqrt(-2.0 * th / (1.0 - th)) * (i_g * xc)
    return a, u


def _scan_rows(a, u, tpos, length):
    d = 1
    while d < length:
        keep = tpos >= d
        a_prev = jnp.where(keep, _shift_rows(a, d), 1.0)
        u_prev = jnp.where(keep, _shift_rows(u, d), 0.0)
        u = a * u_prev + u
        a = a * a_prev
        d *= 2
    return a, u


def _lru_prompt_kernel(*refs, n_tiles):
    y_ref = refs[9]
    tt = pl.program_id(1)

    @pl.when(tt < n_tiles)
    def _():
        _lru_prompt_step(*refs)

    @pl.when(tt >= n_tiles)
    def _():
        y_ref[...] = jnp.zeros(y_ref.shape, y_ref.dtype)


def _lru_prompt_step(x_ref, g_ref, cw_ref, cb_ref, wx_ref, bx_ref, wa_ref, ba_ref, lam_ref,
                     y_ref, hl_ref, tail_ref, hs_ref):
    tt = pl.program_id(1)
    rows = x_ref.shape[0]

    @pl.when(tt == 0)
    def _():
        tail_ref[...] = jnp.zeros(tail_ref.shape, F32)
        hs_ref[...] = jnp.zeros(hs_ref.shape, F32)

    x = x_ref[...]
    w = cw_ref[...]
    tail = tail_ref[...]
    row8 = lax.broadcasted_iota(I32, (8, LRU_BLOCK), 0)
    acc = None
    for i in range(LRU_CONV):
        s = LRU_CONV - 1 - i
        if s == 0:
            xs = x
        else:
            xs = _shift_rows(x, s)
            head = jnp.where(row8 < s, _shift_rows(tail, s), xs[0:8])
            xs = jnp.concatenate([head, xs[8:]], axis=0)
        term = w[i:i + 1] * xs
        acc = term if acc is None else acc + term
    xc = acc + cb_ref[...]
    tail_ref[...] = x[rows - 8:rows]

    a, u = _lru_recurrence_inputs(xc, wx_ref, bx_ref, wa_ref, ba_ref, lam_ref)
    tpos = lax.broadcasted_iota(I32, (rows, 1), 0)
    a, u = _scan_rows(a, u, tpos, rows)
    h = u + a * hs_ref[7:8, :]
    hs_ref[...] = h[rows - 8:rows]
    hl_ref[...] = h[rows - 1:rows]
    y_ref[...] = h * jax.nn.gelu(g_ref[...])


def _lru_sample_kernel(x_ref, g_ref, buf_ref, h0_ref, cw_ref, cb_ref, wx_ref, bx_ref, wa_ref, ba_ref,
                       lam_ref, mix_ref, y_ref, h_ref, *, tq):
    del mix_ref
    rows = x_ref.shape[0]
    x = x_ref[...]
    w = cw_ref[...]
    buf = buf_ref[...]
    tpos = lax.broadcasted_iota(I32, (rows, 1), 0) % tq
    acc = None
    for i in range(LRU_CONV):
        s = LRU_CONV - 1 - i
        xs = x if s == 0 else jnp.where(tpos >= s, _shift_rows(x, s), _shift_rows(buf, -i))
        term = w[i:i + 1] * xs
        acc = term if acc is None else acc + term
    xc = acc + cb_ref[...]
    a, u = _lru_recurrence_inputs(xc, wx_ref, bx_ref, wa_ref, ba_ref, lam_ref)
    a, u = _scan_rows(a, u, tpos, tq)
    h = u + a * h0_ref[...]
    h_ref[...] = h
    y_ref[...] = h * jax.nn.gelu(g_ref[...])


def rglru(proj, mp, conv_state, h0, conv_w, conv_b, wx, bx, wa, ba, lam, *, bs, ts, tt=512):
    m = proj.shape[0]
    ms = bs * ts
    nb = LRU_BLOCKS
    assert mp % tt == 0 and mp % ms == 0 and ts == 8
    vec = lambda v: v.reshape(1, W_LRU)
    cspec = lambda shape, im: pl.BlockSpec(shape, im)
    par_p = [cspec((LRU_CONV, LRU_BLOCK), lambda c, t: (0, c)), cspec((1, LRU_BLOCK), lambda c, t: (0, c)),
             cspec((1, LRU_BLOCK, LRU_BLOCK), lambda c, t: (c, 0, 0)), cspec((1, LRU_BLOCK), lambda c, t: (0, c)),
             cspec((1, LRU_BLOCK, LRU_BLOCK), lambda c, t: (c, 0, 0)), cspec((1, LRU_BLOCK), lambda c, t: (0, c)),
             cspec((1, LRU_BLOCK), lambda c, t: (0, c))]
    params = (conv_w, vec(conv_b), wx, vec(bx), wa, vec(ba), vec(lam))
    npt = mp // tt
    assert m % tt == 0
    y, h_last = pl.pallas_call(
        functools.partial(_lru_prompt_kernel, n_tiles=npt),
        grid=(nb, m // tt),
        in_specs=[pl.BlockSpec((tt, LRU_BLOCK), lambda c, t: (jnp.minimum(t, npt - 1), c)),
                  pl.BlockSpec((tt, LRU_BLOCK), lambda c, t: (jnp.minimum(t, npt - 1), nb + c))] + par_p,
        out_specs=[pl.BlockSpec((tt, LRU_BLOCK), lambda c, t: (t, c)),
                   pl.BlockSpec((1, LRU_BLOCK), lambda c, t: (0, c))],
        out_shape=[jax.ShapeDtypeStruct((m, W_LRU), F32), jax.ShapeDtypeStruct((1, W_LRU), F32)],
        scratch_shapes=[pltpu.VMEM((8, LRU_BLOCK), F32), pltpu.VMEM((8, LRU_BLOCK), F32)],
        compiler_params=_params(("parallel", "arbitrary")),
        name="lru_prompt",
    )(proj, proj, *params)

    buf = jnp.pad(conv_state, ((0, 0), (0, ts - (LRU_CONV - 1)), (0, 0))).reshape(ms, W_LRU)
    h0r = jnp.repeat(h0, ts, axis=0)
    rb = mp // ms
    par_s = [cspec((LRU_CONV, LRU_BLOCK), lambda c: (0, c)), cspec((1, LRU_BLOCK), lambda c: (0, c)),
             cspec((1, LRU_BLOCK, LRU_BLOCK), lambda c: (c, 0, 0)), cspec((1, LRU_BLOCK), lambda c: (0, c)),
             cspec((1, LRU_BLOCK, LRU_BLOCK), lambda c: (c, 0, 0)), cspec((1, LRU_BLOCK), lambda c: (0, c)),
             cspec((1, LRU_BLOCK), lambda c: (0, c))]
    y, h_s = pl.pallas_call(
        functools.partial(_lru_sample_kernel, tq=ts),
        grid=(nb,),
        in_specs=[pl.BlockSpec((ms, LRU_BLOCK), lambda c: (rb, c)),
                  pl.BlockSpec((ms, LRU_BLOCK), lambda c: (rb, nb + c)),
                  pl.BlockSpec((ms, LRU_BLOCK), lambda c: (0, c)),
                  pl.BlockSpec((ms, LRU_BLOCK), lambda c: (0, c))] + par_s
                 + [pl.BlockSpec(memory_space=pl.ANY)],
        out_specs=[pl.BlockSpec((ms, LRU_BLOCK), lambda c: (rb, c)),
                   pl.BlockSpec((ms, LRU_BLOCK), lambda c: (0, c))],
        out_shape=[jax.ShapeDtypeStruct((m, W_LRU), F32), jax.ShapeDtypeStruct((ms, W_LRU), F32)],
        input_output_aliases={4 + len(par_s): 0},
        compiler_params=_params(("parallel",)),
        name="lru_sample",
    )(proj, proj, buf, h0r, *params, y)
    return y, h_last, h_s


def _sconv_kernel(h_ref, prev_ref, buf_ref, w_ref, o_ref, z_ref, *, n_prompt_tiles, tq):
    i = pl.program_id(0)
    rows = h_ref.shape[0]
    n = o_ref.shape[1]
    bg = h_ref[:, 0:n]
    z = h_ref[:, n:2 * n] * h_ref[:, 2 * n:3 * n]
    z_ref[...] = z
    w = w_ref[...]

    def finish(fix):
        acc = None
        for k in range(SC_CONV):
            s = SC_CONV - 1 - k
            zs = z if s == 0 else fix(_shift_rows(z, s), s, k)
            term = w[k:k + 1] * zs
            acc = term if acc is None else acc + term
        o_ref[...] = (bg * acc).astype(o_ref.dtype)

    @pl.when(i < n_prompt_tiles)
    def _():
        pz = prev_ref[:, n:2 * n] * prev_ref[:, 2 * n:3 * n]
        pz = jnp.where(i > 0, pz, 0.0)
        row8 = lax.broadcasted_iota(I32, (8, 1), 0)

        def fix(zs, s, k):
            head = jnp.where(row8 < s, _shift_rows(pz, s), zs[0:8])
            return jnp.concatenate([head, zs[8:]], axis=0)

        finish(fix)

    @pl.when(i >= n_prompt_tiles)
    def _():
        tpos = lax.broadcasted_iota(I32, (rows, 1), 0) % tq
        buf = buf_ref[...]
        finish(lambda zs, s, k: jnp.where(tpos >= s, zs, _shift_rows(buf, -k)))


def short_conv(h3, mp, conv_state, conv_w, *, ts, tm=256):
    m = h3.shape[0]
    n = h3.shape[1] // 3
    ms = m - mp
    assert mp % tm == 0 and ms % tm == 0 and ts == 8
    npt = mp // tm
    buf = jnp.pad(conv_state, ((0, 0), (0, ts - (SC_CONV - 1)), (0, 0))).reshape(ms, n)
    return pl.pallas_call(
        functools.partial(_sconv_kernel, n_prompt_tiles=npt, tq=ts),
        grid=(m // tm,),
        in_specs=[pl.BlockSpec((tm, 3 * n), lambda i: (i, 0)),
                  pl.BlockSpec((8, 3 * n), lambda i: (jnp.maximum(jnp.minimum(i, npt) * (tm // 8) - 1, 0), 0)),
                  pl.BlockSpec((tm, n), lambda i: (jnp.maximum(i - npt, 0), 0)),
                  pl.BlockSpec((SC_CONV, n), lambda i: (0, 0))],
        out_specs=[pl.BlockSpec((tm, n), lambda i: (i, 0)), pl.BlockSpec((tm, n), lambda i: (i, 0))],
        out_shape=[jax.ShapeDtypeStruct((m, n), BF16), jax.ShapeDtypeStruct((m, n), F32)],
        compiler_params=_params(("parallel",)),
        name="short_conv",
    )(h3, h3, buf, conv_w)


def _softmax_plain(s):
    e = jnp.exp(s - jnp.max(s, axis=-1, keepdims=True))
    return e / jnp.sum(e, axis=-1, keepdims=True)


def _xattn_heads(q, kv, o_ref):
    width = MEM_HEADS * MEM_HD
    for h in range(MEM_HEADS):
        sl = slice(h * MEM_HD, (h + 1) * MEM_HD)
        kh = kv[:, sl].astype(BF16)
        vh = kv[:, width + h * MEM_HD:width + (h + 1) * MEM_HD].astype(BF16)
        p = _softmax_plain(_dot_nt(q[:, sl].astype(BF16), kh) * (MEM_HD ** -0.5))
        o_ref[:, sl] = _dot(p.astype(BF16), vh)


def _xattn_prompt_kernel(q_ref, kv_ref, o_ref, *, n_tiles):
    i = pl.program_id(0)

    @pl.when(i < n_tiles)
    def _():
        _xattn_heads(q_ref[...], kv_ref[...], o_ref)

    @pl.when(i >= n_tiles)
    def _():
        o_ref[...] = jnp.zeros(o_ref.shape, o_ref.dtype)


def _xattn_sample_kernel(q_ref, *rest):
    nc = MEM_HD // LANE
    k_ref, v_ref, _, o_ref = rest
    q = q_ref[...]
    tq = q.shape[0]
    n_mem = k_ref.shape[0]
    rows, cols = MEM_HEADS * tq, MEM_HEADS * n_mem

    def slab(ref, c):
        return ref[:, :, c * LANE:(c + 1) * LANE].reshape(cols, LANE).astype(BF16)

    s = None
    for c in range(nc):
        qc = jnp.concatenate([q[:, h * MEM_HD + c * LANE:h * MEM_HD + (c + 1) * LANE]
                              for h in range(MEM_HEADS)], axis=0).astype(BF16)
        part = _dot_nt(qc, slab(k_ref, c))
        s = part if s is None else s + part
    row_head = lax.broadcasted_iota(I32, (rows, 1), 0) // tq
    col_head = lax.broadcasted_iota(I32, (1, cols), 1) % MEM_HEADS
    p = _softmax_rows(s * (MEM_HD ** -0.5), row_head == col_head).astype(BF16)
    for c in range(nc):
        oc = _dot(p, slab(v_ref, c))
        for h in range(MEM_HEADS):
            o_ref[:, h * MEM_HD + c * LANE:h * MEM_HD + (c + 1) * LANE] = oc[h * tq:(h + 1) * tq]


def cross_attention(qm, mp, kv_prompt, kv_cache, layer, *, ts, tm=512):
    m, width = qm.shape
    bs = (m - mp) // ts
    assert mp % tm == 0 and mp % ts == 0 and m % tm == 0
    npt = mp // tm
    o = pl.pallas_call(
        functools.partial(_xattn_prompt_kernel, n_tiles=npt),
        grid=(m // tm,),
        in_specs=[pl.BlockSpec((tm, width), lambda i: (jnp.minimum(i, npt - 1), 0)),
                  pl.BlockSpec(kv_prompt.shape, lambda i: (0, 0))],
        out_specs=pl.BlockSpec((tm, width), lambda i: (i, 0)),
        out_shape=jax.ShapeDtypeStruct((m, width), F32),
        compiler_params=_params(("parallel",)),
        name="xattn_prompt",
    )(qm, kv_prompt)
    rb = mp // ts
    slabs = [pl.BlockSpec((None, None, N_MEM, None, MEM_HEADS, MEM_HD), lambda b, kv=kv: (layer, b, 0, kv, 0, 0))
             for kv in range(2)]
    return pl.pallas_call(
        _xattn_sample_kernel,
        grid=(bs,),
        in_specs=[pl.BlockSpec((ts, width), lambda b: (rb + b, 0))] + slabs
                 + [pl.BlockSpec(memory_space=pl.ANY)],
        out_specs=pl.BlockSpec((ts, width), lambda b: (rb + b, 0)),
        out_shape=jax.ShapeDtypeStruct((m, width), F32),
        input_output_aliases={1 + len(slabs): 0},
        compiler_params=_params(("parallel",)),
        name="xattn_sample",
    )(qm, *([kv_cache] * len(slabs)), o)


def _gate_weight(w_in):
    per = N_GATE // NSA_HKV
    parts = [jnp.pad(w_in[:, N_MAIN0 + h * per:N_MAIN0 + (h + 1) * per], ((0, 0), (0, LANE - per)))
             for h in range(NSA_HKV)]
    return jnp.concatenate(parts, axis=1)


def kernel(x_prompt, x_sample, cache_nsa_kv, state_nsa_win_kv, state_lru_conv, state_lru_h, state_sconv, cache_mem_kv, page_table, mem_prompt, w_in0, lru_conv_w, lru_conv_b, lru_wx, lru_bx, lru_wa, lru_ba, lru_lambda, w_out0, w_in1, sconv_w, w_out1, w_q_mem, w_kv_mem, w_o_mem, w_gu, w_down, ln_g, ln_b):
    bp, tp = x_prompt.shape[:2]
    bs, ts = x_sample.shape[:2]
    assert bp == 1
    mp, ms = bp * tp, bs * ts
    past_len = page_table.shape[1] * PAGE_SIZE
    pos_all = jnp.concatenate([jnp.arange(tp, dtype=I32),
                               jnp.tile(past_len + jnp.arange(ts, dtype=I32), bs)])
    rope_cos, rope_sin = _rope_tables(pos_all)
    xa = jnp.concatenate([x_prompt.reshape(mp, D_MODEL), x_sample.reshape(ms, D_MODEL)], axis=0)
    xa_bf = xa.astype(BF16)
    p_nsa, p_win, p_conv, p_h, p_sc, p_mem = [], [], [], [], [], []
    s_nsa, s_win, s_conv, s_h, s_sc = [], [], [], [], []
    for l in range(DEPTH):
        if l % 2 == 0:
            e = l // 2
            lp = (lru_conv_w[e], lru_conv_b[e], lru_wx[e], lru_bx[e], lru_wa[e], lru_ba[e], lru_lambda[e])
            proj = matmul(xa_bf, w_in0[e], n_cols=N_MAIN0)
            gt = matmul(xa_bf, _gate_weight(w_in0[e]))
            q_bf, kv4, kv_bf, kvw, kw_bf, gates, ch = nsa_prep(proj, gt, rope_cos, rope_sin)
            y_lru, h_p, h_s = rglru(proj, mp, state_lru_conv[e], state_lru_h[e], *lp, bs=bs, ts=ts)
            nq, nkv, nkw = NSA_HQ * NSA_HD, 4 * NSA_KVW, 2 * NSA_KVW
            n_pool = cache_nsa_kv.shape[1]
            o_nsa = nsa_prompt(q_bf, gates, ch, kv_bf, kw_bf, tp)
            o_nsa, win_s = nsa_sample(q_bf[mp:].reshape(bs, ts, nq), gates, kv4, kvw, mp,
                                      cache_nsa_kv.reshape(-1, NSA_HD), e * n_pool, page_table,
                                      state_nsa_win_kv.reshape(-1, NSA_HD), e * bs, state_nsa_win_kv.shape[2],
                                      o_nsa)
            mix = jnp.concatenate([y_lru, o_nsa], axis=1).astype(BF16)
            xb_s = proj[mp:, :W_LRU].reshape(bs, ts, W_LRU)
            n_kv, n_kw = nkv // NSA_HD, nkw // NSA_HD
            p_nsa.append(kv4[:mp * n_kv].reshape(bp, tp, 4, NSA_HKV, NSA_HD))
            p_win.append(kvw[(mp - min(WINDOW, tp)) * n_kw:mp * n_kw].reshape(bp, -1, 2, NSA_HKV, NSA_HD))
            p_conv.append(proj[mp - (LRU_CONV - 1):mp, :W_LRU].reshape(bp, LRU_CONV - 1, W_LRU))
            p_h.append(h_p)
            s_nsa.append(kv4[mp * n_kv:].reshape(bs, ts, 4, NSA_HKV, NSA_HD))
            s_win.append(win_s.reshape(state_nsa_win_kv.shape[1:]))
            s_conv.append(xb_s[:, ts - (LRU_CONV - 1):])
            s_h.append(h_s.reshape(bs, ts, W_LRU)[:, ts - 1])
            w_out = w_out0[e]
        else:
            o = l // 2
            h3 = matmul(xa_bf, w_in1[o])
            mix, z = short_conv(h3, mp, state_sconv[o], sconv_w[o], ts=ts)
            p_sc.append(z[mp - (SC_CONV - 1):mp].reshape(bp, SC_CONV - 1, D_MODEL))
            s_sc.append(z[mp:].reshape(bs, ts, D_MODEL)[:, ts - (SC_CONV - 1):])
            w_out = w_out1[o]
        xa, xa_bf = matmul_postnorm(mix, w_out.astype(BF16), xa, ln_g[l, 0], ln_b[l, 0])
        kv_mem_p = matmul(mem_prompt.reshape(bp * N_MEM, D_MODEL).astype(BF16), w_kv_mem[l])
        p_mem.append(kv_mem_p.reshape(bp, N_MEM, 2, MEM_HEADS, MEM_HD))
        qm = matmul(xa_bf, w_q_mem[l])
        om = cross_attention(qm, mp, kv_mem_p, cache_mem_kv, l, ts=ts)
        xa, xa_bf = matmul_postnorm(om.astype(BF16), w_o_mem[l].astype(BF16), xa, ln_g[l, 1], ln_b[l, 1])
        hf = matmul_swiglu(xa_bf, w_gu[l])
        xa, xa_bf = matmul_postnorm(hf, w_down[l].astype(BF16), xa, ln_g[l, 2], ln_b[l, 2])
    return (xa[:mp].reshape(bp, tp, D_MODEL), xa[mp:].reshape(bs, ts, D_MODEL),
            jnp.stack(p_nsa), jnp.stack(p_win), jnp.stack(p_conv), jnp.stack(p_h), jnp.stack(p_sc),
            jnp.stack(p_mem), jnp.stack(s_nsa), jnp.stack(s_win), jnp.stack(s_conv), jnp.stack(s_h),
            jnp.stack(s_sc))
```

```python
import functools

import numpy as np
import jax
import jax.numpy as jnp
from jax import lax
from jax.experimental import pallas as pl
from jax.experimental.pallas import tpu as pltpu

D_MODEL = 2048
SEQ = 8192
DEPTH = 2
DEC_BATCH = 128
DEC_SEQ = 8
PAGE_SIZE = 128

ALPHA = (2.0 * DEPTH) ** 0.25
LN_EPS = 1e-5
W_LRU = D_MODEL // 2
LRU_CONV = 4
LRU_BLOCKS = 8
LRU_BLOCK = W_LRU // LRU_BLOCKS
LRU_C = 8.0
NSA_HQ = 8
NSA_HKV = 2
NSA_HD = 128
NSA_G = NSA_HQ // NSA_HKV
NSA_KVW = NSA_HKV * NSA_HD
CMP_STRIDE = 16
CMP_LEN = 2 * CMP_STRIDE
SEL_BLOCK = 64
SEL_TOPK = 16
WINDOW = 512
Q_BLOCK = 128
ROPE_THETA = 10000.0
BIG = 1e6
NEG = -1e30
NSA_SCALE = NSA_HD ** -0.5
LOG2_E = 1.4426950408889634
EVEN_SPLITS = [W_LRU, 2 * W_LRU, 2 * W_LRU + NSA_HQ * NSA_HD,
               2 * W_LRU + NSA_HQ * NSA_HD + 4 * NSA_KVW,
               2 * W_LRU + NSA_HQ * NSA_HD + 6 * NSA_KVW]
N_MAIN0 = EVEN_SPLITS[-1]
N_GATE = 3 * NSA_HQ
SC_CONV = 3
N_MEM = 256
MEM_HEADS = 4
MEM_HD = D_MODEL // MEM_HEADS
D_FF = ((8 * D_MODEL + 3 * 256 - 1) // (3 * 256)) * 256

LANE = 128
VMEM_LIMIT = 56 * 1024 * 1024
SEL_TILE = 512
MM_TILE_M = 1024
MM_TILE_N = 1024
MM_MAX_K = 2048
LN_TILE_M = 512
LN_K_SPLIT = 4

F32 = jnp.float32
BF16 = jnp.bfloat16
I32 = jnp.int32


def _params(sem):
    return pltpu.CompilerParams(dimension_semantics=sem, vmem_limit_bytes=VMEM_LIMIT)


def _dot(a, b):
    return jnp.dot(a, b, preferred_element_type=F32)


def _as_bf16(x):
    return x if x.dtype == BF16 else x.astype(BF16)


def _dot_nt(a, b):
    return lax.dot_general(a, b, (((1,), (1,)), ((), ())), preferred_element_type=F32)


def _mm_kernel(x_ref, w_ref, o_ref):
    o_ref[...] = _dot(x_ref[...], _as_bf16(w_ref[...])).astype(o_ref.dtype)


def _pick_tile(n, prefs):
    for t in prefs:
        if n % t == 0:
            return t
    return n


def _weight_spec(w, layer, rows, cols, index_map):
    if w.ndim == 2:
        return pl.BlockSpec((rows, cols), index_map)
    return pl.BlockSpec((None, rows, cols), lambda *g: (layer,) + tuple(index_map(*g)))


def matmul(x, w, *, layer=0, n_cols=None, out_dtype=F32):
    m, kdim = x.shape
    n = w.shape[-1] if n_cols is None else n_cols
    assert x.dtype == BF16
    tm = _pick_tile(m, (MM_TILE_M,))
    tn = _pick_tile(n, (MM_TILE_N, MM_TILE_N // 2, MM_TILE_N // 4))
    return pl.pallas_call(
        _mm_kernel,
        grid=(m // tm, n // tn),
        in_specs=[pl.BlockSpec((tm, kdim), lambda i, j: (i, 0)),
                  _weight_spec(w, layer, kdim, tn, lambda i, j: (0, j))],
        out_specs=pl.BlockSpec((tm, tn), lambda i, j: (i, j)),
        out_shape=jax.ShapeDtypeStruct((m, n), out_dtype),
        compiler_params=_params(("parallel", "parallel")),
        name="matmul",
    )(x, w)


def _mm_ln_kernel(x_ref, w_ref, res_ref, g_ref, b_ref, o0_ref, o1_ref, *acc, nk, n_first):
    part = _dot(x_ref[...], w_ref[...])

    def finish(y):
        z = ALPHA * res_ref[...] + y
        mu = jnp.mean(z, axis=-1, keepdims=True)
        zc = z - mu
        var = jnp.mean(zc * zc, axis=-1, keepdims=True)
        out = zc * lax.rsqrt(var + LN_EPS) * g_ref[...] + b_ref[...]
        if n_first is None:
            o0_ref[...] = out
            o1_ref[...] = out.astype(BF16)
        else:
            i = pl.program_id(0)

            @pl.when(i < n_first)
            def _():
                o0_ref[...] = out

            @pl.when(i >= n_first)
            def _():
                o1_ref[...] = out

    if nk == 1:
        finish(part)
        return
    acc_ref, = acc
    k = pl.program_id(1)

    @pl.when(k == 0)
    def _():
        acc_ref[...] = part

    @pl.when(k > 0)
    def _():
        acc_ref[...] += part

    @pl.when(k == nk - 1)
    def _():
        finish(acc_ref[...])


def matmul_postnorm(x, w, res, g, b, *, layer=0, split_rows=None):
    m, kdim = x.shape
    n = w.shape[-1]
    assert x.dtype == BF16 and w.dtype == BF16 and res.shape == (m, n)
    tm = _pick_tile(m, (LN_TILE_M,))
    nk = 1 if kdim <= MM_MAX_K else LN_K_SPLIT
    assert kdim % (nk * LANE) == 0
    tk = kdim // nk
    row = pl.BlockSpec((tm, n), lambda i, k: (i, 0))
    if split_rows is None:
        n_first = None
        out_specs = [row, row]
        out_shape = [jax.ShapeDtypeStruct((m, n), F32), jax.ShapeDtypeStruct((m, n), BF16)]
    else:
        assert split_rows % tm == 0 and 0 < split_rows < m
        n_first = split_rows // tm
        out_specs = [pl.BlockSpec((tm, n), lambda i, k: (jnp.minimum(i, n_first - 1), 0)),
                     pl.BlockSpec((tm, n), lambda i, k: (jnp.maximum(i - n_first, 0), 0))]
        out_shape = [jax.ShapeDtypeStruct((split_rows, n), F32), jax.ShapeDtypeStruct((m - split_rows, n), F32)]
    return pl.pallas_call(
        functools.partial(_mm_ln_kernel, nk=nk, n_first=n_first),
        grid=(m // tm, nk),
        in_specs=[pl.BlockSpec((tm, tk), lambda i, k: (i, k)),
                  _weight_spec(w, layer, tk, n, lambda i, k: (k, 0)),
                  row,
                  pl.BlockSpec((1, n), lambda i, k: (0, 0)),
                  pl.BlockSpec((1, n), lambda i, k: (0, 0))],
        out_specs=out_specs,
        out_shape=out_shape,
        scratch_shapes=[] if nk == 1 else [pltpu.VMEM((tm, n), F32)],
        compiler_params=_params(("arbitrary", "arbitrary")),
        name="matmul_postnorm",
    )(x, w, res, g.reshape(1, n), b.reshape(1, n))


def _mm_swiglu_kernel(x_ref, wg_ref, wu_ref, o_ref):
    x = x_ref[...]
    g = _dot(x, _as_bf16(wg_ref[...]))
    u = _dot(x, _as_bf16(wu_ref[...]))
    o_ref[...] = (g * jax.nn.sigmoid(g) * u).astype(o_ref.dtype)


def matmul_swiglu(x, w_gu, *, layer=0):
    m, kdim = x.shape
    f = w_gu.shape[-1] // 2
    assert x.dtype == BF16
    tm = _pick_tile(m, (MM_TILE_M,))
    tn = _pick_tile(f, (MM_TILE_N // 2, MM_TILE_N // 4))
    nj = f // tn
    return pl.pallas_call(
        _mm_swiglu_kernel,
        grid=(m // tm, nj),
        in_specs=[pl.BlockSpec((tm, kdim), lambda i, j: (i, 0)),
                  _weight_spec(w_gu, layer, kdim, tn, lambda i, j: (0, j)),
                  _weight_spec(w_gu, layer, kdim, tn, lambda i, j: (0, j + nj))],
        out_specs=pl.BlockSpec((tm, tn), lambda i, j: (i, j)),
        out_shape=jax.ShapeDtypeStruct((m, f), BF16),
        compiler_params=_params(("parallel", "parallel")),
        name="matmul_swiglu",
    )(x, w_gu, w_gu)


def _rope_tables(pos):
    half = NSA_HD // 2
    inv = ROPE_THETA ** (-jnp.arange(half, dtype=F32) / half)
    ang = pos.astype(F32)[:, None] * inv[None, :]
    cos, sin = jnp.cos(ang), jnp.sin(ang)
    return jnp.concatenate([cos, cos], axis=-1), jnp.concatenate([-sin, sin], axis=-1)


def _nsa_prep_kernel(q_ref, kv_ref, kw_ref, gt_ref, cos_ref, sin_ref,
                     qbf_ref, kv_o_ref, kvbf_ref, kw_o_ref, kwbf_ref, g_o_ref, ch_ref):
    cos = cos_ref[...]
    sin = sin_ref[...]
    rows = q_ref.shape[0]

    def rot(x):
        return x * cos + pltpu.roll(x, NSA_HD // 2, axis=1) * sin

    for h in range(NSA_HQ):
        sl = slice(NSA_HD * h, NSA_HD * (h + 1))
        qbf_ref[:, sl] = rot(q_ref[:, sl]).astype(BF16)
    for c in range(4 * NSA_HKV):
        sl = slice(NSA_HD * c, NSA_HD * (c + 1))
        slot = c // NSA_HKV
        x = kv_ref[:, sl]
        if slot in (0, 2):
            x = rot(x)
        kv_o_ref[pl.ds(c, rows, stride=4 * NSA_HKV), :] = x
        kvbf_ref[:, sl] = x.astype(BF16)
        if slot < 2:
            ch_ref[:, sl] = jnp.sum(x.reshape(rows // CMP_STRIDE, CMP_STRIDE, NSA_HD), axis=1)
    for c in range(2 * NSA_HKV):
        sl = slice(NSA_HD * c, NSA_HD * (c + 1))
        x = kw_ref[:, sl]
        if c // NSA_HKV == 0:
            x = rot(x)
        kw_o_ref[pl.ds(c, rows, stride=2 * NSA_HKV), :] = x
        kwbf_ref[:, sl] = x.astype(BF16)
    g_o_ref[...] = jax.nn.sigmoid(gt_ref[...])


def nsa_prep(proj, gt, cos, sin, *, tm=512):
    m = proj.shape[0]
    assert m % tm == 0
    qc = EVEN_SPLITS[1] // (NSA_HQ * NSA_HD)
    kc = EVEN_SPLITS[2] // (4 * NSA_KVW)
    wc = EVEN_SPLITS[3] // (2 * NSA_KVW)
    nq, nkv, nkw, ng = NSA_HQ * NSA_HD, 4 * NSA_KVW, 2 * NSA_KVW, NSA_HKV * LANE
    row = lambda n: pl.BlockSpec((tm, n), lambda i: (i, 0))
    return pl.pallas_call(
        _nsa_prep_kernel,
        grid=(m // tm,),
        in_specs=[pl.BlockSpec((tm, nq), lambda i: (i, qc)),
                  pl.BlockSpec((tm, nkv), lambda i: (i, kc)),
                  pl.BlockSpec((tm, nkw), lambda i: (i, wc)),
                  row(ng), row(NSA_HD), row(NSA_HD)],
        out_specs=[row(nq), pl.BlockSpec((tm * nkv // NSA_HD, NSA_HD), lambda i: (i, 0)), row(nkv),
                   pl.BlockSpec((tm * nkw // NSA_HD, NSA_HD), lambda i: (i, 0)), row(nkw), row(ng),
                   pl.BlockSpec((tm // CMP_STRIDE, 2 * NSA_KVW), lambda i: (i, 0))],
        out_shape=[jax.ShapeDtypeStruct((m, nq), BF16),
                   jax.ShapeDtypeStruct((m * nkv // NSA_HD, NSA_HD), F32),
                   jax.ShapeDtypeStruct((m, nkv), BF16),
                   jax.ShapeDtypeStruct((m * nkw // NSA_HD, NSA_HD), F32),
                   jax.ShapeDtypeStruct((m, nkw), BF16),
                   jax.ShapeDtypeStruct((m, ng), F32),
                   jax.ShapeDtypeStruct((m // CMP_STRIDE, 2 * NSA_KVW), F32)],
        compiler_params=_params(("parallel",)),
        name="nsa_prep",
    )(proj, proj, proj, gt, cos, sin)


def _softmax_rows(s, mask):
    sm = jnp.where(mask, s, NEG)
    e = jnp.where(mask, jnp.exp(sm - jnp.max(sm, axis=-1, keepdims=True)), 0.0)
    return e / jnp.maximum(jnp.sum(e, axis=-1, keepdims=True), 1e-30)


def _block_importance(imp, pmat):
    hi = imp.astype(BF16)
    r1 = imp - hi.astype(F32)
    mid = r1.astype(BF16)
    lo = (r1 - mid.astype(F32)).astype(BF16)
    return _dot(hi, pmat) + _dot(mid, pmat) + _dot(lo, pmat)


def _select_blocks(imp_sel, posq):
    blk = lax.broadcasted_iota(I32, (1, LANE), 1)
    cur = posq // SEL_BLOCK
    forced = (blk == 0) | (blk == cur) | (blk == cur - 1)
    valid = blk * SEL_BLOCK <= posq
    score = jnp.where(valid, jnp.where(forced, BIG, imp_sel), -BIG)
    key = lax.bitcast_convert_type(score, I32)
    key = jnp.where(key < 0, key ^ 0x7FFFFFFF, key)
    key_m1 = key - 1
    cnt = jnp.zeros(score.shape, I32)
    for i in range(LANE):
        thr = jnp.where(blk > i, key_m1, key)
        cnt = cnt + (key[:, i:i + 1] > thr).astype(I32)
    return (cnt < SEL_TOPK).astype(F32)


def _pool_matrix(n_cmp):
    c = np.arange(n_cmp)[:, None]
    j = np.arange(LANE)[None, :]
    return jnp.asarray(((c // 4 == j).astype(np.float32) + ((c + 1) // 4 == j)), dtype=BF16)


def _select_blocks_wide(imp_sel, posq, n_live, key_ref):
    blk = lax.broadcasted_iota(I32, (1, LANE), 1)
    cur = posq // SEL_BLOCK
    forced = (blk == 0) | (blk == cur) | (blk == cur - 1)
    valid = blk * SEL_BLOCK <= posq
    score_t = jnp.where(valid, jnp.where(forced, BIG, imp_sel), -BIG).T
    key = lax.bitcast_convert_type(score_t, I32)
    key = jnp.where(key < 0, key ^ 0x7FFFFFFF, key)
    key_m1 = key - 1
    key_ref[...] = key
    blk_s = lax.broadcasted_iota(I32, (LANE, 1), 0)

    def body(c, cnt):
        base = pl.multiple_of(c * 8, 8)
        cand = key_ref[pl.ds(base, 8), :]
        for k in range(8):
            thr = jnp.where(blk_s > base + k, key_m1, key)
            cnt = cnt + (cand[k:k + 1, :] > thr).astype(I32)
        return cnt

    cnt = lax.fori_loop(0, n_live // 8, body, jnp.zeros((LANE, LANE), I32))
    return (cnt < SEL_TOPK).astype(F32)


def _nsa_prompt_kernel(*refs, n_qb):
    o_ref = refs[9]
    qb = pl.program_id(1)

    @pl.when(qb < n_qb)
    def _():
        _nsa_prompt_step(*refs)

    @pl.when(qb >= n_qb)
    def _():
        o_ref[...] = jnp.zeros(o_ref.shape, o_ref.dtype)


def _nsa_prompt_step(q_ref, g_ref, chk_ref, chv_ref, ks_ref, vs_ref, kw_ref, vw_ref, pmat_ref,
                     o_ref, kct_ref, vc_ref, key_ref):
    qb = pl.program_id(1)
    n_cmp = chk_ref.shape[0]

    @pl.when(qb == 0)
    def _():
        ck = chk_ref[...]
        kc = (ck + pltpu.roll(ck, n_cmp - 1, axis=0)) * (1.0 / CMP_LEN)
        kct_ref[...] = kc.T.astype(BF16)
        cv = chv_ref[...]
        vc_ref[...] = ((cv + pltpu.roll(cv, n_cmp - 1, axis=0)) * (1.0 / CMP_LEN)).astype(BF16)

    qblk = q_ref[...]
    q2 = jnp.concatenate([qblk[:, NSA_HD * g:NSA_HD * (g + 1)] for g in range(NSA_G)], axis=0)
    rows = NSA_G * Q_BLOCK
    posq = qb * Q_BLOCK + lax.broadcasted_iota(I32, (Q_BLOCK, 1), 0)
    posq4 = jnp.concatenate([posq] * NSA_G, axis=0)

    s = _dot(q2, kct_ref[...]) * NSA_SCALE
    cend = lax.broadcasted_iota(I32, (1, n_cmp), 1) * CMP_STRIDE + (CMP_LEN - 1)
    p = _softmax_rows(s, cend <= posq4)
    o_cmp = _dot(p.astype(BF16), vc_ref[...])
    imp = p[0:Q_BLOCK]
    for g in range(1, NSA_G):
        imp = imp + p[g * Q_BLOCK:(g + 1) * Q_BLOCK]
    n_live = ((qb + 1) * (Q_BLOCK // SEL_BLOCK) + 7) // 8 * 8
    sel_t = _select_blocks_wide(_block_importance(imp, pmat_ref[...]), posq, n_live, key_ref).astype(BF16)

    q2t = q2.astype(F32).T.astype(BF16)
    krow = lax.broadcasted_iota(I32, (SEL_TILE, 1), 0)
    bcol = lax.broadcasted_iota(I32, (1, LANE), 1)
    posq_l = qb * Q_BLOCK + bcol

    def tile_update(t, carry):
        m, l, acc = carry
        off = pl.multiple_of(t * SEL_TILE, SEL_TILE)
        k = ks_ref[pl.ds(off, SEL_TILE), :]
        v = vs_ref[pl.ds(off, SEL_TILE), :]
        st = _dot(k, q2t)
        expand = (krow // SEL_BLOCK + t * (SEL_TILE // SEL_BLOCK) == bcol).astype(BF16)
        chosen = _dot(expand, sel_t)
        msk = (chosen > 0.5) & (krow + t * SEL_TILE <= posq_l)
        bias = jnp.where(msk, 0.0, NEG)
        sm = st * (NSA_SCALE * LOG2_E) + jnp.concatenate([bias] * NSA_G, axis=1)
        m_new = jnp.maximum(m, jnp.max(sm, axis=0, keepdims=True))
        alpha = jnp.exp2(m - m_new)
        e = jnp.exp2(sm - m_new)
        l = alpha * l + jnp.sum(e, axis=0, keepdims=True)
        pv = lax.dot_general(v, e.astype(BF16), (((0,), (0,)), ((), ())), preferred_element_type=F32)
        return m_new, l, alpha * acc + pv

    def body(t2, carry):
        return tile_update(2 * t2 + 1, tile_update(2 * t2, carry))

    m0 = jnp.full((1, rows), NEG, F32)
    l0 = jnp.zeros((1, rows), F32)
    n_pairs = (qb * Q_BLOCK) // (2 * SEL_TILE) + 1
    _, l, acc = lax.fori_loop(0, n_pairs, body, (m0, l0, jnp.zeros((NSA_HD, rows), F32)))
    o_sel = (acc / jnp.maximum(l, 1e-30)).T

    span = WINDOW + Q_BLOCK
    start = pl.multiple_of(jnp.maximum(qb - WINDOW // Q_BLOCK, 0) * Q_BLOCK, Q_BLOCK)
    kw = kw_ref[pl.ds(start, span), :]
    vw = vw_ref[pl.ds(start, span), :]
    s = _dot_nt(q2, kw) * NSA_SCALE
    dpos = posq4 - (start + lax.broadcasted_iota(I32, (1, span), 1))
    p = _softmax_rows(s, (dpos >= 0) & (dpos < WINDOW))
    o_win = _dot(p.astype(BF16), vw)

    gates = g_ref[...]
    for g in range(NSA_G):
        rs = slice(g * Q_BLOCK, (g + 1) * Q_BLOCK)
        o_ref[:, NSA_HD * g:NSA_HD * (g + 1)] = (gates[:, 3 * g:3 * g + 1] * o_cmp[rs]
                                                + gates[:, 3 * g + 1:3 * g + 2] * o_sel[rs]
                                                + gates[:, 3 * g + 2:3 * g + 3] * o_win[rs])


def nsa_prompt(q_bf, gates, ch, kv_bf, kw_bf, t):
    m = q_bf.shape[0]
    n_qb = t // Q_BLOCK
    n_cmp = t // CMP_STRIDE
    assert m % Q_BLOCK == 0 and t % (2 * SEL_TILE) == 0
    gw = NSA_G * NSA_HD
    res = lambda c0: pl.BlockSpec((t, NSA_HD), lambda h, i: (0, c0 + h))
    return pl.pallas_call(
        functools.partial(_nsa_prompt_kernel, n_qb=n_qb),
        grid=(NSA_HKV, m // Q_BLOCK),
        in_specs=[pl.BlockSpec((Q_BLOCK, gw), lambda h, i: (jnp.minimum(i, n_qb - 1), h)),
                  pl.BlockSpec((Q_BLOCK, LANE), lambda h, i: (jnp.minimum(i, n_qb - 1), h)),
                  pl.BlockSpec((n_cmp, NSA_HD), lambda h, i: (0, h)),
                  pl.BlockSpec((n_cmp, NSA_HD), lambda h, i: (0, NSA_HKV + h)),
                  res(2 * NSA_HKV), res(3 * NSA_HKV), res(0), res(NSA_HKV),
                  pl.BlockSpec((n_cmp, LANE), lambda h, i: (0, 0))],
        out_specs=pl.BlockSpec((Q_BLOCK, gw), lambda h, i: (i, h)),
        out_shape=jax.ShapeDtypeStruct((m, NSA_HQ * NSA_HD), F32),
        scratch_shapes=[pltpu.VMEM((NSA_HD, n_cmp), BF16), pltpu.VMEM((n_cmp, NSA_HD), BF16),
                        pltpu.VMEM((LANE, LANE), I32)],
        compiler_params=_params(("parallel", "arbitrary")),
        name="nsa_prompt",
    )(q_bf, gates, ch, ch, kv_bf, kv_bf, kw_bf, kw_bf, _pool_matrix(n_cmp))


def _nsa_sample_kernel(pt_ref, q_ref, g_ref, kvn_ref, kwn_ref, win_ref, *rest, n_pages, past_len):
    pages = rest[:n_pages]
    pmat_ref, emat_ref, _, o_ref, nwin_ref = rest[n_pages:]
    tq = q_ref.shape[1]
    n_kv, n_kw = 4 * NSA_HKV, 2 * NSA_HKV
    w_buf = win_ref.shape[0] // n_kw
    pad_rows = PAGE_SIZE - tq
    qf = q_ref[0].astype(F32)
    gates = g_ref[...]

    def rows_of(ref, c, n, per):
        return ref[pl.ds(c, n, stride=per), :]
    posq = past_len + lax.broadcasted_iota(I32, (tq, 1), 0)
    posq4 = jnp.concatenate([posq] * NSA_G, axis=0)
    zpad = jnp.zeros((pad_rows, NSA_HD), F32)

    def new_page(x):
        return jnp.concatenate([x, zpad], axis=0).astype(BF16)

    def chunk_sums(pg, c):
        x = rows_of(pg, c, PAGE_SIZE, n_kv)
        return jnp.sum(x.reshape(PAGE_SIZE // CMP_STRIDE, CMP_STRIDE, NSA_HD), axis=1)

    n_cmp = past_len // CMP_STRIDE
    crow = lax.broadcasted_iota(I32, (n_cmp, 1), 0)

    def compressed(c):
        ch = jnp.concatenate([chunk_sums(pg, c) for pg in pages], axis=0)
        ch_new = jnp.sum(rows_of(kvn_ref, c, tq, n_kv), axis=0, keepdims=True)
        nxt = jnp.where(crow == n_cmp - 1, ch_new, pltpu.roll(ch, n_cmp - 1, axis=0))
        return ((ch + nxt) * (1.0 / CMP_LEN)).astype(BF16)

    for h in range(NSA_HKV):
        q2 = jnp.concatenate([qf[:, NSA_HD * (NSA_G * h + g):NSA_HD * (NSA_G * h + g + 1)]
                              for g in range(NSA_G)], axis=0).astype(BF16)
        col = lambda slot: slot * NSA_HKV + h

        kc = compressed(col(0))
        vc = compressed(col(1))
        s = _dot_nt(q2, kc) * NSA_SCALE
        cend = lax.broadcasted_iota(I32, (1, n_cmp), 1) * CMP_STRIDE + (CMP_LEN - 1)
        p = _softmax_rows(s, cend <= posq4)
        o_cmp = _dot(p.astype(BF16), vc)
        imp = p[0:tq]
        for g in range(1, NSA_G):
            imp = imp + p[g * tq:(g + 1) * tq]
        sel = _select_blocks(_block_importance(imp, pmat_ref[...]), posq)
        sel4 = jnp.concatenate([sel] * NSA_G, axis=0).astype(BF16)

        chosen = _dot(sel4, emat_ref[...])
        pieces = [_dot_nt(q2, rows_of(pg, col(2), PAGE_SIZE, n_kv).astype(BF16)) for pg in pages]
        pieces.append(_dot_nt(q2, new_page(rows_of(kvn_ref, col(2), tq, n_kv))))
        s = jnp.concatenate(pieces, axis=1) * NSA_SCALE
        n_key = s.shape[1]
        kpos = lax.broadcasted_iota(I32, (1, n_key), 1)
        p = _softmax_rows(s, (chosen > 0.5) & (kpos <= posq4)).astype(BF16)
        o_sel = _dot(p[:, past_len:], new_page(rows_of(kvn_ref, col(3), tq, n_kv)))
        for i, pg in enumerate(pages):
            o_sel = o_sel + _dot(p[:, i * PAGE_SIZE:(i + 1) * PAGE_SIZE],
                                 rows_of(pg, col(3), PAGE_SIZE, n_kv).astype(BF16))

        kc_w, vc_w = h, NSA_HKV + h
        s = jnp.concatenate([_dot_nt(q2, rows_of(win_ref, kc_w, w_buf, n_kw).astype(BF16)),
                             _dot_nt(q2, new_page(rows_of(kwn_ref, kc_w, tq, n_kw)))], axis=1) * NSA_SCALE
        posw = (past_len - w_buf) + lax.broadcasted_iota(I32, (1, w_buf + PAGE_SIZE), 1)
        dpos = posq4 - posw
        p = _softmax_rows(s, (dpos >= 0) & (dpos < WINDOW) & (posw >= 0)).astype(BF16)
        o_win = (_dot(p[:, :w_buf], rows_of(win_ref, vc_w, w_buf, n_kw).astype(BF16))
                 + _dot(p[:, w_buf:], new_page(rows_of(kwn_ref, vc_w, tq, n_kw))))

        for g in range(NSA_G):
            rs = slice(g * tq, (g + 1) * tq)
            gc = h * LANE + 3 * g
            head = NSA_G * h + g
            o_ref[:, NSA_HD * head:NSA_HD * (head + 1)] = (
                gates[:, gc:gc + 1] * o_cmp[rs] + gates[:, gc + 1:gc + 2] * o_sel[rs]
                + gates[:, gc + 2:gc + 3] * o_win[rs])

    nwin_ref[0:(w_buf - tq) * n_kw, :] = win_ref[tq * n_kw:w_buf * n_kw, :]
    nwin_ref[(w_buf - tq) * n_kw:w_buf * n_kw, :] = kwn_ref[...]


def nsa_sample(q_bf, gates, kv_new, kw_new, mp, pool, page_base, page_table, win_all, win_base, w_buf, o_buf):
    bsz, tq, nq = q_bf.shape
    n_pages = page_table.shape[1]
    past_len = n_pages * PAGE_SIZE
    n_kv, n_kw = 4 * NSA_HKV, 2 * NSA_HKV
    assert tq == 8 and past_len % SEL_BLOCK == 0 and mp % tq == 0
    n_cmp = past_len // CMP_STRIDE
    n_key = past_len + PAGE_SIZE
    rb = mp // tq
    emat = jnp.asarray(np.arange(LANE)[:, None] == (np.arange(n_key)[None, :] // SEL_BLOCK), dtype=BF16)
    seq = lambda r, n: pl.BlockSpec((r, n), lambda b, pt: (rb + b, 0))
    page = lambda p: pl.BlockSpec((PAGE_SIZE * n_kv, NSA_HD),
                                  lambda b, pt: (page_base + pt[b * n_pages + p], 0))
    const = lambda shape: pl.BlockSpec(shape, lambda b, pt: (0, 0))
    return pl.pallas_call(
        functools.partial(_nsa_sample_kernel, n_pages=n_pages, past_len=past_len),
        grid_spec=pltpu.PrefetchScalarGridSpec(
            num_scalar_prefetch=1,
            grid=(bsz,),
            in_specs=[pl.BlockSpec((1, tq, nq), lambda b, pt: (b, 0, 0)),
                      seq(tq, gates.shape[1]), seq(tq * n_kv, NSA_HD), seq(tq * n_kw, NSA_HD),
                      pl.BlockSpec((w_buf * n_kw, NSA_HD), lambda b, pt: (win_base + b, 0))]
                     + [page(p) for p in range(n_pages)]
                     + [const((n_cmp, LANE)), const((LANE, n_key)), pl.BlockSpec(memory_space=pl.ANY)],
            out_specs=[seq(tq, nq), pl.BlockSpec((w_buf * n_kw, NSA_HD), lambda b, pt: (b, 0))]),
        out_shape=[jax.ShapeDtypeStruct(o_buf.shape, F32),
                   jax.ShapeDtypeStruct((bsz * w_buf * n_kw, NSA_HD), F32)],
        input_output_aliases={8 + n_pages: 0},
        compiler_params=_params(("parallel",)),
        name="nsa_sample",
    )(page_table.reshape(-1), q_bf, gates, kv_new, kw_new, win_all, *([pool] * n_pages),
      _pool_matrix(n_cmp), emat, o_buf)


def _shift_rows(x, s):
    return pltpu.roll(x, s % x.shape[0], axis=0)


def _lru_recurrence_inputs(xc, wx_ref, bx_ref, wa_ref, ba_ref, lam_ref):
    xb = xc.astype(BF16)
    i_g = jax.nn.sigmoid(_dot(xb, wx_ref[0].astype(BF16)) + bx_ref[...])
    r_g = jax.nn.sigmoid(_dot(xb, wa_ref[0].astype(BF16)) + ba_ref[...])
    z = -lam_ref[...]
    softplus = jnp.maximum(z, 0.0) + jnp.log1p(jnp.exp(-jnp.abs(z)))
    log_a = -LRU_C * r_g * softplus
    a = jnp.exp(log_a)
    th = jnp.tanh(log_a)
    u = jnp.sqrt(-2.0 * th / (1.0 - th)) * (i_g * xc)
    return a, u


def _scan_rows(a, u, tpos, length):
    d = 1
    while d < length:
        keep = tpos >= d
        a_prev = jnp.where(keep, _shift_rows(a, d), 1.0)
        u_prev = jnp.where(keep, _shift_rows(u, d), 0.0)
        u = a * u_prev + u
        a = a * a_prev
        d *= 2
    return a, u


def _lru_prompt_kernel(*refs, n_tiles):
    y_ref = refs[9]
    tt = pl.program_id(1)

    @pl.when(tt < n_tiles)
    def _():
        _lru_prompt_step(*refs)

    @pl.when(tt >= n_tiles)
    def _():
        y_ref[...] = jnp.zeros(y_ref.shape, y_ref.dtype)


def _lru_prompt_step(x_ref, g_ref, cw_ref, cb_ref, wx_ref, bx_ref, wa_ref, ba_ref, lam_ref,
                     y_ref, hl_ref, tail_ref, hs_ref):
    tt = pl.program_id(1)
    rows = x_ref.shape[0]

    @pl.when(tt == 0)
    def _():
        tail_ref[...] = jnp.zeros(tail_ref.shape, F32)
        hs_ref[...] = jnp.zeros(hs_ref.shape, F32)

    x = x_ref[...]
    w = cw_ref[...]
    tail = tail_ref[...]
    row8 = lax.broadcasted_iota(I32, (8, LRU_BLOCK), 0)
    acc = None
    for i in range(LRU_CONV):
        s = LRU_CONV - 1 - i
        if s == 0:
            xs = x
        else:
            xs = _shift_rows(x, s)
            head = jnp.where(row8 < s, _shift_rows(tail, s), xs[0:8])
            xs = jnp.concatenate([head, xs[8:]], axis=0)
        term = w[i:i + 1] * xs
        acc = term if acc is None else acc + term
    xc = acc + cb_ref[...]
    tail_ref[...] = x[rows - 8:rows]

    a, u = _lru_recurrence_inputs(xc, wx_ref, bx_ref, wa_ref, ba_ref, lam_ref)
    tpos = lax.broadcasted_iota(I32, (rows, 1), 0)
    a, u = _scan_rows(a, u, tpos, rows)
    h = u + a * hs_ref[7:8, :]
    hs_ref[...] = h[rows - 8:rows]
    hl_ref[...] = h[rows - 1:rows]
    y_ref[...] = h * jax.nn.gelu(g_ref[...])


def _lru_sample_kernel(x_ref, g_ref, buf_ref, h0_ref, cw_ref, cb_ref, wx_ref, bx_ref, wa_ref, ba_ref,
                       lam_ref, mix_ref, y_ref, h_ref, *, tq):
    del mix_ref
    rows = x_ref.shape[0]
    x = x_ref[...]
    w = cw_ref[...]
    buf = buf_ref[...]
    tpos = lax.broadcasted_iota(I32, (rows, 1), 0) % tq
    acc = None
    for i in range(LRU_CONV):
        s = LRU_CONV - 1 - i
        xs = x if s == 0 else jnp.where(tpos >= s, _shift_rows(x, s), _shift_rows(buf, -i))
        term = w[i:i + 1] * xs
        acc = term if acc is None else acc + term
    xc = acc + cb_ref[...]
    a, u = _lru_recurrence_inputs(xc, wx_ref, bx_ref, wa_ref, ba_ref, lam_ref)
    a, u = _scan_rows(a, u, tpos, tq)
    h = u + a * h0_ref[...]
    h_ref[...] = h
    y_ref[...] = h * jax.nn.gelu(g_ref[...])


def rglru(proj, mp, conv_state, h0, conv_w, conv_b, wx, bx, wa, ba, lam, *, bs, ts, tt=512):
    m = proj.shape[0]
    ms = bs * ts
    nb = LRU_BLOCKS
    assert mp % tt == 0 and mp % ms == 0 and ts == 8
    vec = lambda v: v.reshape(1, W_LRU)
    cspec = lambda shape, im: pl.BlockSpec(shape, im)
    par_p = [cspec((LRU_CONV, LRU_BLOCK), lambda c, t: (0, c)), cspec((1, LRU_BLOCK), lambda c, t: (0, c)),
             cspec((1, LRU_BLOCK, LRU_BLOCK), lambda c, t: (c, 0, 0)), cspec((1, LRU_BLOCK), lambda c, t: (0, c)),
             cspec((1, LRU_BLOCK, LRU_BLOCK), lambda c, t: (c, 0, 0)), cspec((1, LRU_BLOCK), lambda c, t: (0, c)),
             cspec((1, LRU_BLOCK), lambda c, t: (0, c))]
    params = (conv_w, vec(conv_b), wx, vec(bx), wa, vec(ba), vec(lam))
    npt = mp // tt
    assert m % tt == 0
    y, h_last = pl.pallas_call(
        functools.partial(_lru_prompt_kernel, n_tiles=npt),
        grid=(nb, m // tt),
        in_specs=[pl.BlockSpec((tt, LRU_BLOCK), lambda c, t: (jnp.minimum(t, npt - 1), c)),
                  pl.BlockSpec((tt, LRU_BLOCK), lambda c, t: (jnp.minimum(t, npt - 1), nb + c))] + par_p,
        out_specs=[pl.BlockSpec((tt, LRU_BLOCK), lambda c, t: (t, c)),
                   pl.BlockSpec((1, LRU_BLOCK), lambda c, t: (0, c))],
        out_shape=[jax.ShapeDtypeStruct((m, W_LRU), F32), jax.ShapeDtypeStruct((1, W_LRU), F32)],
        scratch_shapes=[pltpu.VMEM((8, LRU_BLOCK), F32), pltpu.VMEM((8, LRU_BLOCK), F32)],
        compiler_params=_params(("parallel", "arbitrary")),
        name="lru_prompt",
    )(proj, proj, *params)

    buf = jnp.pad(conv_state, ((0, 0), (0, ts - (LRU_CONV - 1)), (0, 0))).reshape(ms, W_LRU)
    h0r = jnp.repeat(h0, ts, axis=0)
    rb = mp // ms
    par_s = [cspec((LRU_CONV, LRU_BLOCK), lambda c: (0, c)), cspec((1, LRU_BLOCK), lambda c: (0, c)),
             cspec((1, LRU_BLOCK, LRU_BLOCK), lambda c: (c, 0, 0)), cspec((1, LRU_BLOCK), lambda c: (0, c)),
             cspec((1, LRU_BLOCK, LRU_BLOCK), lambda c: (c, 0, 0)), cspec((1, LRU_BLOCK), lambda c: (0, c)),
             cspec((1, LRU_BLOCK), lambda c: (0, c))]
    y, h_s = pl.pallas_call(
        functools.partial(_lru_sample_kernel, tq=ts),
        grid=(nb,),
        in_specs=[pl.BlockSpec((ms, LRU_BLOCK), lambda c: (rb, c)),
                  pl.BlockSpec((ms, LRU_BLOCK), lambda c: (rb, nb + c)),
                  pl.BlockSpec((ms, LRU_BLOCK), lambda c: (0, c)),
                  pl.BlockSpec((ms, LRU_BLOCK), lambda c: (0, c))] + par_s
                 + [pl.BlockSpec(memory_space=pl.ANY)],
        out_specs=[pl.BlockSpec((ms, LRU_BLOCK), lambda c: (rb, c)),
                   pl.BlockSpec((ms, LRU_BLOCK), lambda c: (0, c))],
        out_shape=[jax.ShapeDtypeStruct((m, W_LRU), F32), jax.ShapeDtypeStruct((ms, W_LRU), F32)],
        input_output_aliases={4 + len(par_s): 0},
        compiler_params=_params(("parallel",)),
        name="lru_sample",
    )(proj, proj, buf, h0r, *params, y)
    return y, h_last, h_s


def _sconv_kernel(h_ref, prev_ref, buf_ref, w_ref, o_ref, z_ref, *, n_prompt_tiles, tq):
    i = pl.program_id(0)
    rows = h_ref.shape[0]
    n = o_ref.shape[1]
    bg = h_ref[:, 0:n]
    z = h_ref[:, n:2 * n] * h_ref[:, 2 * n:3 * n]
    z_ref[...] = z
    w = w_ref[...]

    def finish(fix):
        acc = None
        for k in range(SC_CONV):
            s = SC_CONV - 1 - k
            zs = z if s == 0 else fix(_shift_rows(z, s), s, k)
            term = w[k:k + 1] * zs
            acc = term if acc is None else acc + term
        o_ref[...] = (bg * acc).astype(o_ref.dtype)

    @pl.when(i < n_prompt_tiles)
    def _():
        pz = prev_ref[:, n:2 * n] * prev_ref[:, 2 * n:3 * n]
        pz = jnp.where(i > 0, pz, 0.0)
        row8 = lax.broadcasted_iota(I32, (8, 1), 0)

        def fix(zs, s, k):
            head = jnp.where(row8 < s, _shift_rows(pz, s), zs[0:8])
            return jnp.concatenate([head, zs[8:]], axis=0)

        finish(fix)

    @pl.when(i >= n_prompt_tiles)
    def _():
        tpos = lax.broadcasted_iota(I32, (rows, 1), 0) % tq
        buf = buf_ref[...]
        finish(lambda zs, s, k: jnp.where(tpos >= s, zs, _shift_rows(buf, -k)))


def short_conv(h3, mp, conv_state, conv_w, *, ts, tm=256):
    m = h3.shape[0]
    n = h3.shape[1] // 3
    ms = m - mp
    assert mp % tm == 0 and ms % tm == 0 and ts == 8
    npt = mp // tm
    buf = jnp.pad(conv_state, ((0, 0), (0, ts - (SC_CONV - 1)), (0, 0))).reshape(ms, n)
    return pl.pallas_call(
        functools.partial(_sconv_kernel, n_prompt_tiles=npt, tq=ts),
        grid=(m // tm,),
        in_specs=[pl.BlockSpec((tm, 3 * n), lambda i: (i, 0)),
                  pl.BlockSpec((8, 3 * n), lambda i: (jnp.maximum(jnp.minimum(i, npt) * (tm // 8) - 1, 0), 0)),
                  pl.BlockSpec((tm, n), lambda i: (jnp.maximum(i - npt, 0), 0)),
                  pl.BlockSpec((SC_CONV, n), lambda i: (0, 0))],
        out_specs=[pl.BlockSpec((tm, n), lambda i: (i, 0)), pl.BlockSpec((tm, n), lambda i: (i, 0))],
        out_shape=[jax.ShapeDtypeStruct((m, n), BF16), jax.ShapeDtypeStruct((m, n), F32)],
        compiler_params=_params(("parallel",)),
        name="short_conv",
    )(h3, h3, buf, conv_w)


def _softmax_plain(s):
    e = jnp.exp(s - jnp.max(s, axis=-1, keepdims=True))
    return e / jnp.sum(e, axis=-1, keepdims=True)


def _xattn_heads(q, kv, o_ref):
    width = MEM_HEADS * MEM_HD
    for h in range(MEM_HEADS):
        sl = slice(h * MEM_HD, (h + 1) * MEM_HD)
        kh = kv[:, sl].astype(BF16)
        vh = kv[:, width + h * MEM_HD:width + (h + 1) * MEM_HD].astype(BF16)
        p = _softmax_plain(_dot_nt(q[:, sl].astype(BF16), kh) * (MEM_HD ** -0.5))
        o_ref[:, sl] = _dot(p.astype(BF16), vh)


def _xattn_prompt_kernel(q_ref, kv_ref, o_ref, *, n_tiles):
    i = pl.program_id(0)

    @pl.when(i < n_tiles)
    def _():
        _xattn_heads(q_ref[...], kv_ref[...], o_ref)

    @pl.when(i >= n_tiles)
    def _():
        o_ref[...] = jnp.zeros(o_ref.shape, o_ref.dtype)


def _xattn_sample_kernel(q_ref, *rest):
    nc = MEM_HD // LANE
    k_ref, v_ref, _, o_ref = rest
    q = q_ref[...]
    tq = q.shape[0]
    n_mem = k_ref.shape[0]
    rows, cols = MEM_HEADS * tq, MEM_HEADS * n_mem

    def slab(ref, c):
        return ref[:, :, c * LANE:(c + 1) * LANE].reshape(cols, LANE).astype(BF16)

    s = None
    for c in range(nc):
        qc = jnp.concatenate([q[:, h * MEM_HD + c * LANE:h * MEM_HD + (c + 1) * LANE]
                              for h in range(MEM_HEADS)], axis=0).astype(BF16)
        part = _dot_nt(qc, slab(k_ref, c))
        s = part if s is None else s + part
    row_head = lax.broadcasted_iota(I32, (rows, 1), 0) // tq
    col_head = lax.broadcasted_iota(I32, (1, cols), 1) % MEM_HEADS
    p = _softmax_rows(s * (MEM_HD ** -0.5), row_head == col_head).astype(BF16)
    for c in range(nc):
        oc = _dot(p, slab(v_ref, c))
        for h in range(MEM_HEADS):
            o_ref[:, h * MEM_HD + c * LANE:h * MEM_HD + (c + 1) * LANE] = oc[h * tq:(h + 1) * tq]


def cross_attention(qm, mp, kv_prompt, kv_cache, layer, *, ts, tm=512):
    m, width = qm.shape
    bs = (m - mp) // ts
    assert mp % tm == 0 and mp % ts == 0 and m % tm == 0
    npt = mp // tm
    o = pl.pallas_call(
        functools.partial(_xattn_prompt_kernel, n_tiles=npt),
        grid=(m // tm,),
        in_specs=[pl.BlockSpec((tm, width), lambda i: (jnp.minimum(i, npt - 1), 0)),
                  pl.BlockSpec(kv_prompt.shape, lambda i: (0, 0))],
        out_specs=pl.BlockSpec((tm, width), lambda i: (i, 0)),
        out_shape=jax.ShapeDtypeStruct((m, width), F32),
        compiler_params=_params(("parallel",)),
        name="xattn_prompt",
    )(qm, kv_prompt)
    rb = mp // ts
    slabs = [pl.BlockSpec((None, None, N_MEM, None, MEM_HEADS, MEM_HD), lambda b, kv=kv: (layer, b, 0, kv, 0, 0))
             for kv in range(2)]
    return pl.pallas_call(
        _xattn_sample_kernel,
        grid=(bs,),
        in_specs=[pl.BlockSpec((ts, width), lambda b: (rb + b, 0))] + slabs
                 + [pl.BlockSpec(memory_space=pl.ANY)],
        out_specs=pl.BlockSpec((ts, width), lambda b: (rb + b, 0)),
        out_shape=jax.ShapeDtypeStruct((m, width), F32),
        input_output_aliases={1 + len(slabs): 0},
        compiler_params=_params(("parallel",)),
        name="xattn_sample",
    )(qm, *([kv_cache] * len(slabs)), o)


def _gate_weight(w_in):
    per = N_GATE // NSA_HKV
    parts = [jnp.pad(w_in[:, N_MAIN0 + h * per:N_MAIN0 + (h + 1) * per], ((0, 0), (0, LANE - per)))
             for h in range(NSA_HKV)]
    return jnp.concatenate(parts, axis=1)


def kernel(x_prompt, x_sample, cache_nsa_kv, state_nsa_win_kv, state_lru_conv, state_lru_h, state_sconv, cache_mem_kv, page_table, mem_prompt, w_in0, lru_conv_w, lru_conv_b, lru_wx, lru_bx, lru_wa, lru_ba, lru_lambda, w_out0, w_in1, sconv_w, w_out1, w_q_mem, w_kv_mem, w_o_mem, w_gu, w_down, ln_g, ln_b):
    bp, tp = x_prompt.shape[:2]
    bs, ts = x_sample.shape[:2]
    assert bp == 1
    mp, ms = bp * tp, bs * ts
    past_len = page_table.shape[1] * PAGE_SIZE
    pos_all = jnp.concatenate([jnp.arange(tp, dtype=I32),
                               jnp.tile(past_len + jnp.arange(ts, dtype=I32), bs)])
    rope_cos, rope_sin = _rope_tables(pos_all)
    xa = jnp.concatenate([x_prompt.reshape(mp, D_MODEL), x_sample.reshape(ms, D_MODEL)], axis=0)
    xa_bf = xa.astype(BF16)
    w_out0_bf, w_out1_bf = w_out0.astype(BF16), w_out1.astype(BF16)
    w_o_mem_bf, w_down_bf = w_o_mem.astype(BF16), w_down.astype(BF16)
    p_nsa, p_win, p_conv, p_h, p_sc, p_mem = [], [], [], [], [], []
    s_nsa, s_win, s_conv, s_h, s_sc = [], [], [], [], []
    for l in range(DEPTH):
        if l % 2 == 0:
            e = l // 2
            lp = (lru_conv_w[e], lru_conv_b[e], lru_wx[e], lru_bx[e], lru_wa[e], lru_ba[e], lru_lambda[e])
            proj = matmul(xa_bf, w_in0, layer=e, n_cols=N_MAIN0)
            gt = matmul(xa_bf, _gate_weight(w_in0[e]))
            q_bf, kv4, kv_bf, kvw, kw_bf, gates, ch = nsa_prep(proj, gt, rope_cos, rope_sin)
            y_lru, h_p, h_s = rglru(proj, mp, state_lru_conv[e], state_lru_h[e], *lp, bs=bs, ts=ts)
            nq, nkv, nkw = NSA_HQ * NSA_HD, 4 * NSA_KVW, 2 * NSA_KVW
            n_pool = cache_nsa_kv.shape[1]
            o_nsa = nsa_prompt(q_bf, gates, ch, kv_bf, kw_bf, tp)
            o_nsa, win_s = nsa_sample(q_bf[mp:].reshape(bs, ts, nq), gates, kv4, kvw, mp,
                                      cache_nsa_kv.reshape(-1, NSA_HD), e * n_pool, page_table,
                                      state_nsa_win_kv.reshape(-1, NSA_HD), e * bs, state_nsa_win_kv.shape[2],
                                      o_nsa)
            mix = jnp.concatenate([y_lru, o_nsa], axis=1).astype(BF16)
            xb_s = proj[mp:, :W_LRU].reshape(bs, ts, W_LRU)
            n_kv, n_kw = nkv // NSA_HD, nkw // NSA_HD
            p_nsa.append(kv4[:mp * n_kv].reshape(bp, tp, 4, NSA_HKV, NSA_HD))
            p_win.append(kvw[(mp - min(WINDOW, tp)) * n_kw:mp * n_kw].reshape(bp, -1, 2, NSA_HKV, NSA_HD))
            p_conv.append(proj[mp - (LRU_CONV - 1):mp, :W_LRU].reshape(bp, LRU_CONV - 1, W_LRU))
            p_h.append(h_p)
            s_nsa.append(kv4[mp * n_kv:].reshape(bs, ts, 4, NSA_HKV, NSA_HD))
            s_win.append(win_s.reshape(state_nsa_win_kv.shape[1:]))
            s_conv.append(xb_s[:, ts - (LRU_CONV - 1):])
            s_h.append(h_s.reshape(bs, ts, W_LRU)[:, ts - 1])
            w_out, l_out = w_out0_bf, e
        else:
            o = l // 2
            h3 = matmul(xa_bf, w_in1, layer=o)
            mix, z = short_conv(h3, mp, state_sconv[o], sconv_w[o], ts=ts)
            p_sc.append(z[mp - (SC_CONV - 1):mp].reshape(bp, SC_CONV - 1, D_MODEL))
            s_sc.append(z[mp:].reshape(bs, ts, D_MODEL)[:, ts - (SC_CONV - 1):])
            w_out, l_out = w_out1_bf, o
        xa, xa_bf = matmul_postnorm(mix, w_out, xa, ln_g[l, 0], ln_b[l, 0], layer=l_out)
        kv_mem_p = matmul(mem_prompt.reshape(bp * N_MEM, D_MODEL).astype(BF16), w_kv_mem, layer=l)
        p_mem.append(kv_mem_p.reshape(bp, N_MEM, 2, MEM_HEADS, MEM_HD))
        qm = matmul(xa_bf, w_q_mem, layer=l)
        om = cross_attention(qm, mp, kv_mem_p, cache_mem_kv, l, ts=ts)
        xa, xa_bf = matmul_postnorm(om.astype(BF16), w_o_mem_bf, xa, ln_g[l, 1], ln_b[l, 1], layer=l)
        hf = matmul_swiglu(xa_bf, w_gu, layer=l)
        if l < DEPTH - 1:
            xa, xa_bf = matmul_postnorm(hf, w_down_bf, xa, ln_g[l, 2], ln_b[l, 2], layer=l)
        else:
            y_prompt, y_sample = matmul_postnorm(hf, w_down_bf, xa, ln_g[l, 2], ln_b[l, 2], layer=l,
                                                 split_rows=mp)
    return (y_prompt.reshape(bp, tp, D_MODEL), y_sample.reshape(bs, ts, D_MODEL),
            jnp.stack(p_nsa), jnp.stack(p_win), jnp.stack(p_conv), jnp.stack(p_h), jnp.stack(p_sc),
            jnp.stack(p_mem), jnp.stack(s_nsa), jnp.stack(s_win), jnp.stack(s_conv), jnp.stack(s_h),
            jnp.stack(s_sc))
```

```python
import functools

import numpy as np
import jax
import jax.numpy as jnp
from jax import lax
from jax.experimental import pallas as pl
from jax.experimental.pallas import tpu as pltpu

D_MODEL = 2048
SEQ = 8192
DEPTH = 2
DEC_BATCH = 128
DEC_SEQ = 8
PAGE_SIZE = 128

ALPHA = (2.0 * DEPTH) ** 0.25
LN_EPS = 1e-5
W_LRU = D_MODEL // 2
LRU_CONV = 4
LRU_BLOCKS = 8
LRU_BLOCK = W_LRU // LRU_BLOCKS
LRU_C = 8.0
NSA_HQ = 8
NSA_HKV = 2
NSA_HD = 128
NSA_G = NSA_HQ // NSA_HKV
NSA_KVW = NSA_HKV * NSA_HD
CMP_STRIDE = 16
CMP_LEN = 2 * CMP_STRIDE
SEL_BLOCK = 64
SEL_TOPK = 16
WINDOW = 512
Q_BLOCK = 128
ROPE_THETA = 10000.0
BIG = 1e6
NEG = -1e30
NSA_SCALE = NSA_HD ** -0.5
LOG2_E = 1.4426950408889634
EVEN_SPLITS = [W_LRU, 2 * W_LRU, 2 * W_LRU + NSA_HQ * NSA_HD,
               2 * W_LRU + NSA_HQ * NSA_HD + 4 * NSA_KVW,
               2 * W_LRU + NSA_HQ * NSA_HD + 6 * NSA_KVW]
N_MAIN0 = EVEN_SPLITS[-1]
N_GATE = 3 * NSA_HQ
SC_CONV = 3
N_MEM = 256
MEM_HEADS = 4
MEM_HD = D_MODEL // MEM_HEADS
D_FF = ((8 * D_MODEL + 3 * 256 - 1) // (3 * 256)) * 256

LANE = 128
VMEM_LIMIT = 56 * 1024 * 1024
SEL_TILE = 512
MM_TILE_M = 1024
MM_TILE_N = 1024
MM_MAX_K = 2048
LN_TILE_M = 512
XATTN_SEQS = 2
NSA_SEQS = 2
LN_K_SPLIT = 4

F32 = jnp.float32
BF16 = jnp.bfloat16
I32 = jnp.int32


def _params(sem):
    return pltpu.CompilerParams(dimension_semantics=sem, vmem_limit_bytes=VMEM_LIMIT)


def _dot(a, b):
    return jnp.dot(a, b, preferred_element_type=F32)


def _as_bf16(x):
    return x if x.dtype == BF16 else x.astype(BF16)


def _dot_nt(a, b):
    return lax.dot_general(a, b, (((1,), (1,)), ((), ())), preferred_element_type=F32)


def _mm_kernel(x_ref, w_ref, o_ref):
    o_ref[...] = _dot(x_ref[...], _as_bf16(w_ref[...])).astype(o_ref.dtype)


def _pick_tile(n, prefs):
    for t in prefs:
        if n % t == 0:
            return t
    return n


def _weight_spec(w, layer, rows, cols, index_map):
    if w.ndim == 2:
        return pl.BlockSpec((rows, cols), index_map)
    return pl.BlockSpec((None, rows, cols), lambda *g: (layer,) + tuple(index_map(*g)))


def matmul(x, w, *, layer=0, n_cols=None, out_dtype=F32):
    m, kdim = x.shape
    n = w.shape[-1] if n_cols is None else n_cols
    assert x.dtype == BF16
    tm = _pick_tile(m, (MM_TILE_M,))
    tn = _pick_tile(n, (MM_TILE_N, MM_TILE_N // 2, MM_TILE_N // 4))
    return pl.pallas_call(
        _mm_kernel,
        grid=(m // tm, n // tn),
        in_specs=[pl.BlockSpec((tm, kdim), lambda i, j: (i, 0)),
                  _weight_spec(w, layer, kdim, tn, lambda i, j: (0, j))],
        out_specs=pl.BlockSpec((tm, tn), lambda i, j: (i, j)),
        out_shape=jax.ShapeDtypeStruct((m, n), out_dtype),
        compiler_params=_params(("parallel", "parallel")),
        name="matmul",
    )(x, w)


def _mm_ln_kernel(x_ref, w_ref, res_ref, g_ref, b_ref, o0_ref, o1_ref, *acc, nk, n_first):
    part = _dot(x_ref[...], w_ref[...])

    def finish(y):
        z = ALPHA * res_ref[...] + y
        mu = jnp.mean(z, axis=-1, keepdims=True)
        zc = z - mu
        var = jnp.mean(zc * zc, axis=-1, keepdims=True)
        out = zc * lax.rsqrt(var + LN_EPS) * g_ref[...] + b_ref[...]
        if n_first is None:
            o0_ref[...] = out
            o1_ref[...] = out.astype(BF16)
        else:
            i = pl.program_id(0)

            @pl.when(i < n_first)
            def _():
                o0_ref[...] = out

            @pl.when(i >= n_first)
            def _():
                o1_ref[...] = out

    if nk == 1:
        finish(part)
        return
    acc_ref, = acc
    k = pl.program_id(1)

    @pl.when(k == 0)
    def _():
        acc_ref[...] = part

    @pl.when(k > 0)
    def _():
        acc_ref[...] += part

    @pl.when(k == nk - 1)
    def _():
        finish(acc_ref[...])


def matmul_postnorm(x, w, res, g, b, *, layer=0, split_rows=None):
    m, kdim = x.shape
    n = w.shape[-1]
    assert x.dtype == BF16 and w.dtype == BF16 and res.shape == (m, n)
    tm = _pick_tile(m, (LN_TILE_M,))
    nk = 1 if kdim <= MM_MAX_K else LN_K_SPLIT
    assert kdim % (nk * LANE) == 0
    tk = kdim // nk
    row = pl.BlockSpec((tm, n), lambda i, k: (i, 0))
    if split_rows is None:
        n_first = None
        out_specs = [row, row]
        out_shape = [jax.ShapeDtypeStruct((m, n), F32), jax.ShapeDtypeStruct((m, n), BF16)]
    else:
        assert split_rows % tm == 0 and 0 < split_rows < m
        n_first = split_rows // tm
        out_specs = [pl.BlockSpec((tm, n), lambda i, k: (jnp.minimum(i, n_first - 1), 0)),
                     pl.BlockSpec((tm, n), lambda i, k: (jnp.maximum(i - n_first, 0), 0))]
        out_shape = [jax.ShapeDtypeStruct((split_rows, n), F32), jax.ShapeDtypeStruct((m - split_rows, n), F32)]
    return pl.pallas_call(
        functools.partial(_mm_ln_kernel, nk=nk, n_first=n_first),
        grid=(m // tm, nk),
        in_specs=[pl.BlockSpec((tm, tk), lambda i, k: (i, k)),
                  _weight_spec(w, layer, tk, n, lambda i, k: (k, 0)),
                  row,
                  pl.BlockSpec((1, n), lambda i, k: (0, 0)),
                  pl.BlockSpec((1, n), lambda i, k: (0, 0))],
        out_specs=out_specs,
        out_shape=out_shape,
        scratch_shapes=[] if nk == 1 else [pltpu.VMEM((tm, n), F32)],
        compiler_params=_params(("arbitrary", "arbitrary")),
        name="matmul_postnorm",
    )(x, w, res, g.reshape(1, n), b.reshape(1, n))


def _mm_swiglu_kernel(x_ref, wg_ref, wu_ref, o_ref):
    x = x_ref[...]
    g = _dot(x, _as_bf16(wg_ref[...]))
    u = _dot(x, _as_bf16(wu_ref[...]))
    o_ref[...] = (g * jax.nn.sigmoid(g) * u).astype(o_ref.dtype)


def matmul_swiglu(x, w_gu, *, layer=0):
    m, kdim = x.shape
    f = w_gu.shape[-1] // 2
    assert x.dtype == BF16
    tm = _pick_tile(m, (MM_TILE_M,))
    tn = _pick_tile(f, (MM_TILE_N // 2, MM_TILE_N // 4))
    nj = f // tn
    return pl.pallas_call(
        _mm_swiglu_kernel,
        grid=(m // tm, nj),
        in_specs=[pl.BlockSpec((tm, kdim), lambda i, j: (i, 0)),
                  _weight_spec(w_gu, layer, kdim, tn, lambda i, j: (0, j)),
                  _weight_spec(w_gu, layer, kdim, tn, lambda i, j: (0, j + nj))],
        out_specs=pl.BlockSpec((tm, tn), lambda i, j: (i, j)),
        out_shape=jax.ShapeDtypeStruct((m, f), BF16),
        compiler_params=_params(("parallel", "parallel")),
        name="matmul_swiglu",
    )(x, w_gu, w_gu)


def _rope_tables(pos):
    half = NSA_HD // 2
    inv = ROPE_THETA ** (-jnp.arange(half, dtype=F32) / half)
    ang = pos.astype(F32)[:, None] * inv[None, :]
    cos, sin = jnp.cos(ang), jnp.sin(ang)
    return jnp.concatenate([cos, cos], axis=-1), jnp.concatenate([-sin, sin], axis=-1)


def _nsa_prep_kernel(q_ref, kv_ref, kw_ref, gt_ref, cos_ref, sin_ref,
                     qbf_ref, kv_o_ref, kvbf_ref, kw_o_ref, kwbf_ref, g_o_ref, ch_ref):
    cos = cos_ref[...]
    sin = sin_ref[...]
    rows = q_ref.shape[0]

    def rot(x):
        return x * cos + pltpu.roll(x, NSA_HD // 2, axis=1) * sin

    for h in range(NSA_HQ):
        sl = slice(NSA_HD * h, NSA_HD * (h + 1))
        qbf_ref[:, sl] = rot(q_ref[:, sl]).astype(BF16)
    for c in range(4 * NSA_HKV):
        sl = slice(NSA_HD * c, NSA_HD * (c + 1))
        slot = c // NSA_HKV
        x = kv_ref[:, sl]
        if slot in (0, 2):
            x = rot(x)
        kv_o_ref[pl.ds(c, rows, stride=4 * NSA_HKV), :] = x
        kvbf_ref[:, sl] = x.astype(BF16)
        if slot < 2:
            ch_ref[:, sl] = jnp.sum(x.reshape(rows // CMP_STRIDE, CMP_STRIDE, NSA_HD), axis=1)
    for c in range(2 * NSA_HKV):
        sl = slice(NSA_HD * c, NSA_HD * (c + 1))
        x = kw_ref[:, sl]
        if c // NSA_HKV == 0:
            x = rot(x)
        kw_o_ref[pl.ds(c, rows, stride=2 * NSA_HKV), :] = x
        kwbf_ref[:, sl] = x.astype(BF16)
    g_o_ref[...] = jax.nn.sigmoid(gt_ref[...])


def nsa_prep(proj, gt, cos, sin, *, tm=512):
    m = proj.shape[0]
    assert m % tm == 0
    qc = EVEN_SPLITS[1] // (NSA_HQ * NSA_HD)
    kc = EVEN_SPLITS[2] // (4 * NSA_KVW)
    wc = EVEN_SPLITS[3] // (2 * NSA_KVW)
    nq, nkv, nkw, ng = NSA_HQ * NSA_HD, 4 * NSA_KVW, 2 * NSA_KVW, NSA_HKV * LANE
    row = lambda n: pl.BlockSpec((tm, n), lambda i: (i, 0))
    return pl.pallas_call(
        _nsa_prep_kernel,
        grid=(m // tm,),
        in_specs=[pl.BlockSpec((tm, nq), lambda i: (i, qc)),
                  pl.BlockSpec((tm, nkv), lambda i: (i, kc)),
                  pl.BlockSpec((tm, nkw), lambda i: (i, wc)),
                  row(ng), row(NSA_HD), row(NSA_HD)],
        out_specs=[row(nq), pl.BlockSpec((tm * nkv // NSA_HD, NSA_HD), lambda i: (i, 0)), row(nkv),
                   pl.BlockSpec((tm * nkw // NSA_HD, NSA_HD), lambda i: (i, 0)), row(nkw), row(ng),
                   pl.BlockSpec((tm // CMP_STRIDE, 2 * NSA_KVW), lambda i: (i, 0))],
        out_shape=[jax.ShapeDtypeStruct((m, nq), BF16),
                   jax.ShapeDtypeStruct((m * nkv // NSA_HD, NSA_HD), F32),
                   jax.ShapeDtypeStruct((m, nkv), BF16),
                   jax.ShapeDtypeStruct((m * nkw // NSA_HD, NSA_HD), F32),
                   jax.ShapeDtypeStruct((m, nkw), BF16),
                   jax.ShapeDtypeStruct((m, ng), F32),
                   jax.ShapeDtypeStruct((m // CMP_STRIDE, 2 * NSA_KVW), F32)],
        compiler_params=_params(("parallel",)),
        name="nsa_prep",
    )(proj, proj, proj, gt, cos, sin)


def _softmax_rows(s, mask):
    sm = jnp.where(mask, s, NEG)
    e = jnp.where(mask, jnp.exp(sm - jnp.max(sm, axis=-1, keepdims=True)), 0.0)
    return e / jnp.maximum(jnp.sum(e, axis=-1, keepdims=True), 1e-30)


def _block_importance(imp, pmat):
    hi = imp.astype(BF16)
    r1 = imp - hi.astype(F32)
    mid = r1.astype(BF16)
    lo = (r1 - mid.astype(F32)).astype(BF16)
    return _dot(hi, pmat) + _dot(mid, pmat) + _dot(lo, pmat)


def _select_blocks(imp_sel, posq):
    blk = lax.broadcasted_iota(I32, (1, LANE), 1)
    cur = posq // SEL_BLOCK
    forced = (blk == 0) | (blk == cur) | (blk == cur - 1)
    valid = blk * SEL_BLOCK <= posq
    score = jnp.where(valid, jnp.where(forced, BIG, imp_sel), -BIG)
    key = lax.bitcast_convert_type(score, I32)
    key = jnp.where(key < 0, key ^ 0x7FFFFFFF, key)
    key_m1 = key - 1
    cnt = jnp.zeros(score.shape, I32)
    for i in range(LANE):
        thr = jnp.where(blk > i, key_m1, key)
        cnt = cnt + (key[:, i:i + 1] > thr).astype(I32)
    return (cnt < SEL_TOPK).astype(F32)


def _pool_matrix(n_cmp):
    c = np.arange(n_cmp)[:, None]
    j = np.arange(LANE)[None, :]
    return jnp.asarray(((c // 4 == j).astype(np.float32) + ((c + 1) // 4 == j)), dtype=BF16)


def _select_blocks_wide(imp_sel, posq, n_live, key_ref):
    blk = lax.broadcasted_iota(I32, (1, LANE), 1)
    cur = posq // SEL_BLOCK
    forced = (blk == 0) | (blk == cur) | (blk == cur - 1)
    valid = blk * SEL_BLOCK <= posq
    score_t = jnp.where(valid, jnp.where(forced, BIG, imp_sel), -BIG).T
    key = lax.bitcast_convert_type(score_t, I32)
    key = jnp.where(key < 0, key ^ 0x7FFFFFFF, key)
    key_m1 = key - 1
    key_ref[...] = key
    blk_s = lax.broadcasted_iota(I32, (LANE, 1), 0)

    def body(c, cnt):
        base = pl.multiple_of(c * 8, 8)
        cand = key_ref[pl.ds(base, 8), :]
        for k in range(8):
            thr = jnp.where(blk_s > base + k, key_m1, key)
            cnt = cnt + (cand[k:k + 1, :] > thr).astype(I32)
        return cnt

    cnt = lax.fori_loop(0, n_live // 8, body, jnp.zeros((LANE, LANE), I32))
    return (cnt < SEL_TOPK).astype(F32)


def _nsa_prompt_kernel(*refs, n_qb):
    o_ref = refs[9]
    qb = pl.program_id(1)

    @pl.when(qb < n_qb)
    def _():
        _nsa_prompt_step(*refs)

    @pl.when(qb >= n_qb)
    def _():
        o_ref[...] = jnp.zeros(o_ref.shape, o_ref.dtype)


def _nsa_prompt_step(q_ref, g_ref, chk_ref, chv_ref, ks_ref, vs_ref, kw_ref, vw_ref, pmat_ref,
                     o_ref, kct_ref, vc_ref, key_ref):
    qb = pl.program_id(1)
    n_cmp = chk_ref.shape[0]

    @pl.when(qb == 0)
    def _():
        ck = chk_ref[...]
        kc = (ck + pltpu.roll(ck, n_cmp - 1, axis=0)) * (1.0 / CMP_LEN)
        kct_ref[...] = kc.T.astype(BF16)
        cv = chv_ref[...]
        vc_ref[...] = ((cv + pltpu.roll(cv, n_cmp - 1, axis=0)) * (1.0 / CMP_LEN)).astype(BF16)

    qblk = q_ref[...]
    q2 = jnp.concatenate([qblk[:, NSA_HD * g:NSA_HD * (g + 1)] for g in range(NSA_G)], axis=0)
    rows = NSA_G * Q_BLOCK
    posq = qb * Q_BLOCK + lax.broadcasted_iota(I32, (Q_BLOCK, 1), 0)
    posq4 = jnp.concatenate([posq] * NSA_G, axis=0)

    s = _dot(q2, kct_ref[...]) * NSA_SCALE
    cend = lax.broadcasted_iota(I32, (1, n_cmp), 1) * CMP_STRIDE + (CMP_LEN - 1)
    p = _softmax_rows(s, cend <= posq4)
    o_cmp = _dot(p.astype(BF16), vc_ref[...])
    imp = p[0:Q_BLOCK]
    for g in range(1, NSA_G):
        imp = imp + p[g * Q_BLOCK:(g + 1) * Q_BLOCK]
    n_live = ((qb + 1) * (Q_BLOCK // SEL_BLOCK) + 7) // 8 * 8
    sel_t = _select_blocks_wide(_block_importance(imp, pmat_ref[...]), posq, n_live, key_ref).astype(BF16)

    q2t = q2.astype(F32).T.astype(BF16)
    krow = lax.broadcasted_iota(I32, (SEL_TILE, 1), 0)
    bcol = lax.broadcasted_iota(I32, (1, LANE), 1)
    posq_l = qb * Q_BLOCK + bcol

    def tile_update(t, carry):
        m, l, acc = carry
        off = pl.multiple_of(t * SEL_TILE, SEL_TILE)
        k = ks_ref[pl.ds(off, SEL_TILE), :]
        v = vs_ref[pl.ds(off, SEL_TILE), :]
        st = _dot(k, q2t)
        expand = (krow // SEL_BLOCK + t * (SEL_TILE // SEL_BLOCK) == bcol).astype(BF16)
        chosen = _dot(expand, sel_t)
        msk = (chosen > 0.5) & (krow + t * SEL_TILE <= posq_l)
        bias = jnp.where(msk, 0.0, NEG)
        sm = st * (NSA_SCALE * LOG2_E) + jnp.concatenate([bias] * NSA_G, axis=1)
        m_new = jnp.maximum(m, jnp.max(sm, axis=0, keepdims=True))
        alpha = jnp.exp2(m - m_new)
        e = jnp.exp2(sm - m_new)
        l = alpha * l + jnp.sum(e, axis=0, keepdims=True)
        pv = lax.dot_general(v, e.astype(BF16), (((0,), (0,)), ((), ())), preferred_element_type=F32)
        return m_new, l, alpha * acc + pv

    def body(t2, carry):
        return tile_update(2 * t2 + 1, tile_update(2 * t2, carry))

    m0 = jnp.full((1, rows), NEG, F32)
    l0 = jnp.zeros((1, rows), F32)
    n_pairs = (qb * Q_BLOCK) // (2 * SEL_TILE) + 1
    _, l, acc = lax.fori_loop(0, n_pairs, body, (m0, l0, jnp.zeros((NSA_HD, rows), F32)))
    o_sel = (acc / jnp.maximum(l, 1e-30)).T

    span = WINDOW + Q_BLOCK
    start = pl.multiple_of(jnp.maximum(qb - WINDOW // Q_BLOCK, 0) * Q_BLOCK, Q_BLOCK)
    kw = kw_ref[pl.ds(start, span), :]
    vw = vw_ref[pl.ds(start, span), :]
    s = _dot_nt(q2, kw) * NSA_SCALE
    dpos = posq4 - (start + lax.broadcasted_iota(I32, (1, span), 1))
    p = _softmax_rows(s, (dpos >= 0) & (dpos < WINDOW))
    o_win = _dot(p.astype(BF16), vw)

    gates = g_ref[...]
    for g in range(NSA_G):
        rs = slice(g * Q_BLOCK, (g + 1) * Q_BLOCK)
        o_ref[:, NSA_HD * g:NSA_HD * (g + 1)] = (gates[:, 3 * g:3 * g + 1] * o_cmp[rs]
                                                + gates[:, 3 * g + 1:3 * g + 2] * o_sel[rs]
                                                + gates[:, 3 * g + 2:3 * g + 3] * o_win[rs]).astype(o_ref.dtype)


def nsa_prompt(q_bf, gates, ch, kv_bf, kw_bf, t):
    m = q_bf.shape[0]
    n_qb = t // Q_BLOCK
    n_cmp = t // CMP_STRIDE
    assert m % Q_BLOCK == 0 and t % (2 * SEL_TILE) == 0
    gw = NSA_G * NSA_HD
    res = lambda c0: pl.BlockSpec((t, NSA_HD), lambda h, i: (0, c0 + h))
    return pl.pallas_call(
        functools.partial(_nsa_prompt_kernel, n_qb=n_qb),
        grid=(NSA_HKV, m // Q_BLOCK),
        in_specs=[pl.BlockSpec((Q_BLOCK, gw), lambda h, i: (jnp.minimum(i, n_qb - 1), h)),
                  pl.BlockSpec((Q_BLOCK, LANE), lambda h, i: (jnp.minimum(i, n_qb - 1), h)),
                  pl.BlockSpec((n_cmp, NSA_HD), lambda h, i: (0, h)),
                  pl.BlockSpec((n_cmp, NSA_HD), lambda h, i: (0, NSA_HKV + h)),
                  res(2 * NSA_HKV), res(3 * NSA_HKV), res(0), res(NSA_HKV),
                  pl.BlockSpec((n_cmp, LANE), lambda h, i: (0, 0))],
        out_specs=pl.BlockSpec((Q_BLOCK, gw), lambda h, i: (i, h)),
        out_shape=jax.ShapeDtypeStruct((m, NSA_HQ * NSA_HD), BF16),
        scratch_shapes=[pltpu.VMEM((NSA_HD, n_cmp), BF16), pltpu.VMEM((n_cmp, NSA_HD), BF16),
                        pltpu.VMEM((LANE, LANE), I32)],
        compiler_params=_params(("parallel", "arbitrary")),
        name="nsa_prompt",
    )(q_bf, gates, ch, ch, kv_bf, kv_bf, kw_bf, kw_bf, _pool_matrix(n_cmp))


def _nsa_sample_kernel(pt_ref, q_ref, g_ref, kvn_ref, kwn_ref, win_ref, *rest, n_pages, past_len):
    n_seq, tq = q_ref.shape[:2]
    all_pages = rest[:n_seq * n_pages]
    pmat_ref, emat_ref, _, o_ref, nwin_ref = rest[n_seq * n_pages:]
    n_kv, n_kw = 4 * NSA_HKV, 2 * NSA_HKV
    w_buf = win_ref.shape[0] // (n_seq * n_kw)
    pad_rows = PAGE_SIZE - tq

    def rows_of(ref, c, n, per, base=0):
        return ref[pl.ds(base + c, n, stride=per), :]

    posq = past_len + lax.broadcasted_iota(I32, (tq, 1), 0)
    posq4 = jnp.concatenate([posq] * NSA_G, axis=0)
    zpad = jnp.zeros((pad_rows, NSA_HD), F32)

    def new_page(x):
        return jnp.concatenate([x, zpad], axis=0).astype(BF16)

    def chunk_sums(pg, c):
        x = rows_of(pg, c, PAGE_SIZE, n_kv)
        return jnp.sum(x.reshape(PAGE_SIZE // CMP_STRIDE, CMP_STRIDE, NSA_HD), axis=1)

    n_cmp = past_len // CMP_STRIDE
    crow = lax.broadcasted_iota(I32, (n_cmp, 1), 0)
    seq_out = []
    for sq in range(n_seq):
        pages = all_pages[sq * n_pages:(sq + 1) * n_pages]
        qf = q_ref[sq].astype(F32)
        gates = g_ref[sq * tq:(sq + 1) * tq, :]
        kv_base, kw_base, win_base = sq * tq * n_kv, sq * tq * n_kw, sq * w_buf * n_kw
        new_kv = lambda c: rows_of(kvn_ref, c, tq, n_kv, kv_base)
        new_kw = lambda c: rows_of(kwn_ref, c, tq, n_kw, kw_base)
        win = lambda c: rows_of(win_ref, c, w_buf, n_kw, win_base)

        def compressed(c):
            ch = jnp.concatenate([chunk_sums(pg, c) for pg in pages], axis=0)
            ch_new = jnp.sum(new_kv(c), axis=0, keepdims=True)
            nxt = jnp.where(crow == n_cmp - 1, ch_new, pltpu.roll(ch, n_cmp - 1, axis=0))
            return ((ch + nxt) * (1.0 / CMP_LEN)).astype(BF16)

        heads = []
        for h in range(NSA_HKV):
            q2 = jnp.concatenate([qf[:, NSA_HD * (NSA_G * h + g):NSA_HD * (NSA_G * h + g + 1)]
                                  for g in range(NSA_G)], axis=0).astype(BF16)
            col = lambda slot: slot * NSA_HKV + h

            kc = compressed(col(0))
            vc = compressed(col(1))
            s = _dot_nt(q2, kc) * NSA_SCALE
            cend = lax.broadcasted_iota(I32, (1, n_cmp), 1) * CMP_STRIDE + (CMP_LEN - 1)
            p = _softmax_rows(s, cend <= posq4)
            o_cmp = _dot(p.astype(BF16), vc)
            imp = p[0:tq]
            for g in range(1, NSA_G):
                imp = imp + p[g * tq:(g + 1) * tq]
            sel = _select_blocks(_block_importance(imp, pmat_ref[...]), posq)
            sel4 = jnp.concatenate([sel] * NSA_G, axis=0).astype(BF16)

            chosen = _dot(sel4, emat_ref[...])
            pieces = [_dot_nt(q2, rows_of(pg, col(2), PAGE_SIZE, n_kv).astype(BF16)) for pg in pages]
            pieces.append(_dot_nt(q2, new_page(new_kv(col(2)))))
            s = jnp.concatenate(pieces, axis=1) * NSA_SCALE
            n_key = s.shape[1]
            kpos = lax.broadcasted_iota(I32, (1, n_key), 1)
            p = _softmax_rows(s, (chosen > 0.5) & (kpos <= posq4)).astype(BF16)
            o_sel = _dot(p[:, past_len:], new_page(new_kv(col(3))))
            for i, pg in enumerate(pages):
                o_sel = o_sel + _dot(p[:, i * PAGE_SIZE:(i + 1) * PAGE_SIZE],
                                     rows_of(pg, col(3), PAGE_SIZE, n_kv).astype(BF16))

            kc_w, vc_w = h, NSA_HKV + h
            s = jnp.concatenate([_dot_nt(q2, win(kc_w).astype(BF16)),
                                 _dot_nt(q2, new_page(new_kw(kc_w)))], axis=1) * NSA_SCALE
            posw = (past_len - w_buf) + lax.broadcasted_iota(I32, (1, w_buf + PAGE_SIZE), 1)
            dpos = posq4 - posw
            p = _softmax_rows(s, (dpos >= 0) & (dpos < WINDOW) & (posw >= 0)).astype(BF16)
            o_win = _dot(p[:, :w_buf], win(vc_w).astype(BF16)) + _dot(p[:, w_buf:], new_page(new_kw(vc_w)))

            for g in range(NSA_G):
                rs = slice(g * tq, (g + 1) * tq)
                gc = h * LANE + 3 * g
                heads.append(gates[:, gc:gc + 1] * o_cmp[rs] + gates[:, gc + 1:gc + 2] * o_sel[rs]
                             + gates[:, gc + 2:gc + 3] * o_win[rs])
        seq_out.append(jnp.concatenate(heads, axis=1))

        rows = w_buf * n_kw
        nwin_ref[win_base:win_base + rows - tq * n_kw, :] = win_ref[win_base + tq * n_kw:win_base + rows, :]
        nwin_ref[win_base + rows - tq * n_kw:win_base + rows, :] = kwn_ref[kw_base:kw_base + tq * n_kw, :]

    o_ref[...] = jnp.concatenate(seq_out, axis=0).astype(o_ref.dtype)


def nsa_sample(q_bf, gates, kv_new, kw_new, mp, pool, page_base, page_table, win_all, win_base, w_buf, o_buf):
    bsz, tq, nq = q_bf.shape
    n_pages = page_table.shape[1]
    past_len = n_pages * PAGE_SIZE
    n_kv, n_kw = 4 * NSA_HKV, 2 * NSA_HKV
    ns = NSA_SEQS
    assert tq == 8 and past_len % SEL_BLOCK == 0 and mp % (ns * tq) == 0
    assert bsz % ns == 0 and win_base % ns == 0
    n_cmp = past_len // CMP_STRIDE
    n_key = past_len + PAGE_SIZE
    rb = mp // (ns * tq)
    emat = jnp.asarray(np.arange(LANE)[:, None] == (np.arange(n_key)[None, :] // SEL_BLOCK), dtype=BF16)
    seq = lambda r, n: pl.BlockSpec((ns * r, n), lambda b, pt: (rb + b, 0))
    page = lambda sq, p: pl.BlockSpec((PAGE_SIZE * n_kv, NSA_HD),
                                      lambda b, pt: (page_base + pt[(ns * b + sq) * n_pages + p], 0))
    const = lambda shape: pl.BlockSpec(shape, lambda b, pt: (0, 0))
    win_rows = ns * w_buf * n_kw
    return pl.pallas_call(
        functools.partial(_nsa_sample_kernel, n_pages=n_pages, past_len=past_len),
        grid_spec=pltpu.PrefetchScalarGridSpec(
            num_scalar_prefetch=1,
            grid=(bsz // ns,),
            in_specs=[pl.BlockSpec((ns, tq, nq), lambda b, pt: (b, 0, 0)),
                      seq(tq, gates.shape[1]), seq(tq * n_kv, NSA_HD), seq(tq * n_kw, NSA_HD),
                      pl.BlockSpec((win_rows, NSA_HD), lambda b, pt: (win_base // ns + b, 0))]
                     + [page(sq, p) for sq in range(ns) for p in range(n_pages)]
                     + [const((n_cmp, LANE)), const((LANE, n_key)), pl.BlockSpec(memory_space=pl.ANY)],
            out_specs=[seq(tq, nq), pl.BlockSpec((win_rows, NSA_HD), lambda b, pt: (b, 0))]),
        out_shape=[jax.ShapeDtypeStruct(o_buf.shape, o_buf.dtype),
                   jax.ShapeDtypeStruct((bsz * w_buf * n_kw, NSA_HD), F32)],
        input_output_aliases={8 + ns * n_pages: 0},
        compiler_params=_params(("parallel",)),
        name="nsa_sample",
    )(page_table.reshape(-1), q_bf, gates, kv_new, kw_new, win_all, *([pool] * (ns * n_pages)),
      _pool_matrix(n_cmp), emat, o_buf)


def _shift_rows(x, s):
    return pltpu.roll(x, s % x.shape[0], axis=0)


def _lru_recurrence_inputs(xc, wx_ref, bx_ref, wa_ref, ba_ref, lam_ref):
    xb = xc.astype(BF16)
    i_g = jax.nn.sigmoid(_dot(xb, wx_ref[0].astype(BF16)) + bx_ref[...])
    r_g = jax.nn.sigmoid(_dot(xb, wa_ref[0].astype(BF16)) + ba_ref[...])
    z = -lam_ref[...]
    softplus = jnp.maximum(z, 0.0) + jnp.log1p(jnp.exp(-jnp.abs(z)))
    log_a = -LRU_C * r_g * softplus
    a = jnp.exp(log_a)
    th = jnp.tanh(log_a)
    u = jnp.sqrt(-2.0 * th / (1.0 - th)) * (i_g * xc)
    return a, u


def _scan_rows(a, u, tpos, length):
    d = 1
    while d < length:
        keep = tpos >= d
        a_prev = jnp.where(keep, _shift_rows(a, d), 1.0)
        u_prev = jnp.where(keep, _shift_rows(u, d), 0.0)
        u = a * u_prev + u
        a = a * a_prev
        d *= 2
    return a, u


def _lru_prompt_kernel(*refs, n_tiles):
    y_ref = refs[9]
    tt = pl.program_id(1)

    @pl.when(tt < n_tiles)
    def _():
        _lru_prompt_step(*refs)

    @pl.when(tt >= n_tiles)
    def _():
        y_ref[...] = jnp.zeros(y_ref.shape, y_ref.dtype)


def _lru_prompt_step(x_ref, g_ref, cw_ref, cb_ref, wx_ref, bx_ref, wa_ref, ba_ref, lam_ref,
                     y_ref, hl_ref, tail_ref, hs_ref):
    tt = pl.program_id(1)
    rows = x_ref.shape[0]

    @pl.when(tt == 0)
    def _():
        tail_ref[...] = jnp.zeros(tail_ref.shape, F32)
        hs_ref[...] = jnp.zeros(hs_ref.shape, F32)

    x = x_ref[...]
    w = cw_ref[...]
    tail = tail_ref[...]
    row8 = lax.broadcasted_iota(I32, (8, LRU_BLOCK), 0)
    acc = None
    for i in range(LRU_CONV):
        s = LRU_CONV - 1 - i
        if s == 0:
            xs = x
        else:
            xs = _shift_rows(x, s)
            head = jnp.where(row8 < s, _shift_rows(tail, s), xs[0:8])
            xs = jnp.concatenate([head, xs[8:]], axis=0)
        term = w[i:i + 1] * xs
        acc = term if acc is None else acc + term
    xc = acc + cb_ref[...]
    tail_ref[...] = x[rows - 8:rows]

    a, u = _lru_recurrence_inputs(xc, wx_ref, bx_ref, wa_ref, ba_ref, lam_ref)
    tpos = lax.broadcasted_iota(I32, (rows, 1), 0)
    a, u = _scan_rows(a, u, tpos, rows)
    h = u + a * hs_ref[7:8, :]
    hs_ref[...] = h[rows - 8:rows]
    hl_ref[...] = h[rows - 1:rows]
    y_ref[...] = (h * jax.nn.gelu(g_ref[...])).astype(y_ref.dtype)


def _lru_sample_kernel(x_ref, g_ref, buf_ref, h0_ref, cw_ref, cb_ref, wx_ref, bx_ref, wa_ref, ba_ref,
                       lam_ref, mix_ref, y_ref, h_ref, *, tq):
    del mix_ref
    rows = x_ref.shape[0]
    x = x_ref[...]
    w = cw_ref[...]
    buf = buf_ref[...]
    tpos = lax.broadcasted_iota(I32, (rows, 1), 0) % tq
    acc = None
    for i in range(LRU_CONV):
        s = LRU_CONV - 1 - i
        xs = x if s == 0 else jnp.where(tpos >= s, _shift_rows(x, s), _shift_rows(buf, -i))
        term = w[i:i + 1] * xs
        acc = term if acc is None else acc + term
    xc = acc + cb_ref[...]
    a, u = _lru_recurrence_inputs(xc, wx_ref, bx_ref, wa_ref, ba_ref, lam_ref)
    a, u = _scan_rows(a, u, tpos, tq)
    h = u + a * h0_ref[...]
    h_ref[...] = h
    y_ref[...] = (h * jax.nn.gelu(g_ref[...])).astype(y_ref.dtype)


def rglru(proj, mp, conv_state, h0, conv_w, conv_b, wx, bx, wa, ba, lam, *, bs, ts, tt=512):
    m = proj.shape[0]
    ms = bs * ts
    nb = LRU_BLOCKS
    assert mp % tt == 0 and mp % ms == 0 and ts == 8
    vec = lambda v: v.reshape(1, W_LRU)
    cspec = lambda shape, im: pl.BlockSpec(shape, im)
    par_p = [cspec((LRU_CONV, LRU_BLOCK), lambda c, t: (0, c)), cspec((1, LRU_BLOCK), lambda c, t: (0, c)),
             cspec((1, LRU_BLOCK, LRU_BLOCK), lambda c, t: (c, 0, 0)), cspec((1, LRU_BLOCK), lambda c, t: (0, c)),
             cspec((1, LRU_BLOCK, LRU_BLOCK), lambda c, t: (c, 0, 0)), cspec((1, LRU_BLOCK), lambda c, t: (0, c)),
             cspec((1, LRU_BLOCK), lambda c, t: (0, c))]
    params = (conv_w, vec(conv_b), wx, vec(bx), wa, vec(ba), vec(lam))
    npt = mp // tt
    assert m % tt == 0
    y, h_last = pl.pallas_call(
        functools.partial(_lru_prompt_kernel, n_tiles=npt),
        grid=(nb, m // tt),
        in_specs=[pl.BlockSpec((tt, LRU_BLOCK), lambda c, t: (jnp.minimum(t, npt - 1), c)),
                  pl.BlockSpec((tt, LRU_BLOCK), lambda c, t: (jnp.minimum(t, npt - 1), nb + c))] + par_p,
        out_specs=[pl.BlockSpec((tt, LRU_BLOCK), lambda c, t: (t, c)),
                   pl.BlockSpec((1, LRU_BLOCK), lambda c, t: (0, c))],
        out_shape=[jax.ShapeDtypeStruct((m, W_LRU), BF16), jax.ShapeDtypeStruct((1, W_LRU), F32)],
        scratch_shapes=[pltpu.VMEM((8, LRU_BLOCK), F32), pltpu.VMEM((8, LRU_BLOCK), F32)],
        compiler_params=_params(("parallel", "arbitrary")),
        name="lru_prompt",
    )(proj, proj, *params)

    buf = jnp.pad(conv_state, ((0, 0), (0, ts - (LRU_CONV - 1)), (0, 0))).reshape(ms, W_LRU)
    h0r = jnp.repeat(h0, ts, axis=0)
    rb = mp // ms
    par_s = [cspec((LRU_CONV, LRU_BLOCK), lambda c: (0, c)), cspec((1, LRU_BLOCK), lambda c: (0, c)),
             cspec((1, LRU_BLOCK, LRU_BLOCK), lambda c: (c, 0, 0)), cspec((1, LRU_BLOCK), lambda c: (0, c)),
             cspec((1, LRU_BLOCK, LRU_BLOCK), lambda c: (c, 0, 0)), cspec((1, LRU_BLOCK), lambda c: (0, c)),
             cspec((1, LRU_BLOCK), lambda c: (0, c))]
    y, h_s = pl.pallas_call(
        functools.partial(_lru_sample_kernel, tq=ts),
        grid=(nb,),
        in_specs=[pl.BlockSpec((ms, LRU_BLOCK), lambda c: (rb, c)),
                  pl.BlockSpec((ms, LRU_BLOCK), lambda c: (rb, nb + c)),
                  pl.BlockSpec((ms, LRU_BLOCK), lambda c: (0, c)),
                  pl.BlockSpec((ms, LRU_BLOCK), lambda c: (0, c))] + par_s
                 + [pl.BlockSpec(memory_space=pl.ANY)],
        out_specs=[pl.BlockSpec((ms, LRU_BLOCK), lambda c: (rb, c)),
                   pl.BlockSpec((ms, LRU_BLOCK), lambda c: (0, c))],
        out_shape=[jax.ShapeDtypeStruct((m, W_LRU), BF16), jax.ShapeDtypeStruct((ms, W_LRU), F32)],
        input_output_aliases={4 + len(par_s): 0},
        compiler_params=_params(("parallel",)),
        name="lru_sample",
    )(proj, proj, buf, h0r, *params, y)
    return y, h_last, h_s


def _sconv_kernel(h_ref, prev_ref, buf_ref, w_ref, o_ref, z_ref, *, n_prompt_tiles, tq):
    i = pl.program_id(0)
    rows = h_ref.shape[0]
    n = o_ref.shape[1]
    bg = h_ref[:, 0:n]
    z = h_ref[:, n:2 * n] * h_ref[:, 2 * n:3 * n]
    z_ref[...] = z
    w = w_ref[...]

    def finish(fix):
        acc = None
        for k in range(SC_CONV):
            s = SC_CONV - 1 - k
            zs = z if s == 0 else fix(_shift_rows(z, s), s, k)
            term = w[k:k + 1] * zs
            acc = term if acc is None else acc + term
        o_ref[...] = (bg * acc).astype(o_ref.dtype)

    @pl.when(i < n_prompt_tiles)
    def _():
        pz = prev_ref[:, n:2 * n] * prev_ref[:, 2 * n:3 * n]
        pz = jnp.where(i > 0, pz, 0.0)
        row8 = lax.broadcasted_iota(I32, (8, 1), 0)

        def fix(zs, s, k):
            head = jnp.where(row8 < s, _shift_rows(pz, s), zs[0:8])
            return jnp.concatenate([head, zs[8:]], axis=0)

        finish(fix)

    @pl.when(i >= n_prompt_tiles)
    def _():
        tpos = lax.broadcasted_iota(I32, (rows, 1), 0) % tq
        buf = buf_ref[...]
        finish(lambda zs, s, k: jnp.where(tpos >= s, zs, _shift_rows(buf, -k)))


def short_conv(h3, mp, conv_state, conv_w, *, ts, tm=256):
    m = h3.shape[0]
    n = h3.shape[1] // 3
    ms = m - mp
    assert mp % tm == 0 and ms % tm == 0 and ts == 8
    npt = mp // tm
    buf = jnp.pad(conv_state, ((0, 0), (0, ts - (SC_CONV - 1)), (0, 0))).reshape(ms, n)
    return pl.pallas_call(
        functools.partial(_sconv_kernel, n_prompt_tiles=npt, tq=ts),
        grid=(m // tm,),
        in_specs=[pl.BlockSpec((tm, 3 * n), lambda i: (i, 0)),
                  pl.BlockSpec((8, 3 * n), lambda i: (jnp.maximum(jnp.minimum(i, npt) * (tm // 8) - 1, 0), 0)),
                  pl.BlockSpec((tm, n), lambda i: (jnp.maximum(i - npt, 0), 0)),
                  pl.BlockSpec((SC_CONV, n), lambda i: (0, 0))],
        out_specs=[pl.BlockSpec((tm, n), lambda i: (i, 0)), pl.BlockSpec((tm, n), lambda i: (i, 0))],
        out_shape=[jax.ShapeDtypeStruct((m, n), BF16), jax.ShapeDtypeStruct((m, n), F32)],
        compiler_params=_params(("parallel",)),
        name="short_conv",
    )(h3, h3, buf, conv_w)


def _softmax_plain(s):
    e = jnp.exp(s - jnp.max(s, axis=-1, keepdims=True))
    return e / jnp.sum(e, axis=-1, keepdims=True)


def _xattn_heads(q, kv, o_ref):
    width = MEM_HEADS * MEM_HD
    for h in range(MEM_HEADS):
        sl = slice(h * MEM_HD, (h + 1) * MEM_HD)
        kh = kv[:, sl].astype(BF16)
        vh = kv[:, width + h * MEM_HD:width + (h + 1) * MEM_HD].astype(BF16)
        p = _softmax_plain(_dot_nt(q[:, sl].astype(BF16), kh) * (MEM_HD ** -0.5))
        o_ref[:, sl] = _dot(p.astype(BF16), vh).astype(o_ref.dtype)


def _xattn_prompt_kernel(q_ref, kv_ref, o_ref, *, n_tiles):
    i = pl.program_id(0)

    @pl.when(i < n_tiles)
    def _():
        _xattn_heads(q_ref[...], kv_ref[...], o_ref)

    @pl.when(i >= n_tiles)
    def _():
        o_ref[...] = jnp.zeros(o_ref.shape, o_ref.dtype)


def _xattn_sample_kernel(q_ref, k_ref, v_ref, buf_ref, o_ref, *, tq):
    del buf_ref
    nc = MEM_HD // LANE
    n_seq, n_mem = k_ref.shape[:2]
    rows, cols = MEM_HEADS * tq, MEM_HEADS * n_mem
    row_head = lax.broadcasted_iota(I32, (rows, 1), 0) // tq
    col_head = lax.broadcasted_iota(I32, (1, cols), 1) % MEM_HEADS
    outs = []
    for i in range(n_seq):
        q = q_ref[i * tq:(i + 1) * tq, :]

        def slab(ref, c):
            return ref[i, :, :, c * LANE:(c + 1) * LANE].reshape(cols, LANE).astype(BF16)

        s = None
        for c in range(nc):
            qc = jnp.concatenate([q[:, h * MEM_HD + c * LANE:h * MEM_HD + (c + 1) * LANE]
                                  for h in range(MEM_HEADS)], axis=0).astype(BF16)
            part = _dot_nt(qc, slab(k_ref, c))
            s = part if s is None else s + part
        p = _softmax_rows(s * (MEM_HD ** -0.5), row_head == col_head).astype(BF16)
        oc = [_dot(p, slab(v_ref, c)) for c in range(nc)]
        outs.append(jnp.concatenate([oc[c][h * tq:(h + 1) * tq] for h in range(MEM_HEADS) for c in range(nc)],
                                    axis=1))
    o_ref[...] = jnp.concatenate(outs, axis=0).astype(o_ref.dtype)


def cross_attention(qm, mp, kv_prompt, kv_cache, layer, *, ts, tm=512):
    m, width = qm.shape
    bs = (m - mp) // ts
    assert mp % tm == 0 and mp % ts == 0 and m % tm == 0
    npt = mp // tm
    o = pl.pallas_call(
        functools.partial(_xattn_prompt_kernel, n_tiles=npt),
        grid=(m // tm,),
        in_specs=[pl.BlockSpec((tm, width), lambda i: (jnp.minimum(i, npt - 1), 0)),
                  pl.BlockSpec(kv_prompt.shape, lambda i: (0, 0))],
        out_specs=pl.BlockSpec((tm, width), lambda i: (i, 0)),
        out_shape=jax.ShapeDtypeStruct((m, width), BF16),
        compiler_params=_params(("parallel",)),
        name="xattn_prompt",
    )(qm, kv_prompt)
    rows = XATTN_SEQS * ts
    assert bs % XATTN_SEQS == 0 and mp % rows == 0
    rb = mp // rows
    slabs = [pl.BlockSpec((None, XATTN_SEQS, N_MEM, None, MEM_HEADS, MEM_HD),
                          lambda b, kv=kv: (layer, b, 0, kv, 0, 0)) for kv in range(2)]
    return pl.pallas_call(
        functools.partial(_xattn_sample_kernel, tq=ts),
        grid=(bs // XATTN_SEQS,),
        in_specs=[pl.BlockSpec((rows, width), lambda b: (rb + b, 0))] + slabs
                 + [pl.BlockSpec(memory_space=pl.ANY)],
        out_specs=pl.BlockSpec((rows, width), lambda b: (rb + b, 0)),
        out_shape=jax.ShapeDtypeStruct((m, width), BF16),
        input_output_aliases={1 + len(slabs): 0},
        compiler_params=_params(("parallel",)),
        name="xattn_sample",
    )(qm, *([kv_cache] * len(slabs)), o)


def _gate_weight(w_in):
    per = N_GATE // NSA_HKV
    parts = [jnp.pad(w_in[:, N_MAIN0 + h * per:N_MAIN0 + (h + 1) * per], ((0, 0), (0, LANE - per)))
             for h in range(NSA_HKV)]
    return jnp.concatenate(parts, axis=1)


def kernel(x_prompt, x_sample, cache_nsa_kv, state_nsa_win_kv, state_lru_conv, state_lru_h, state_sconv, cache_mem_kv, page_table, mem_prompt, w_in0, lru_conv_w, lru_conv_b, lru_wx, lru_bx, lru_wa, lru_ba, lru_lambda, w_out0, w_in1, sconv_w, w_out1, w_q_mem, w_kv_mem, w_o_mem, w_gu, w_down, ln_g, ln_b):
    bp, tp = x_prompt.shape[:2]
    bs, ts = x_sample.shape[:2]
    assert bp == 1
    mp, ms = bp * tp, bs * ts
    past_len = page_table.shape[1] * PAGE_SIZE
    pos_all = jnp.concatenate([jnp.arange(tp, dtype=I32),
                               jnp.tile(past_len + jnp.arange(ts, dtype=I32), bs)])
    rope_cos, rope_sin = _rope_tables(pos_all)
    xa = jnp.concatenate([x_prompt.reshape(mp, D_MODEL), x_sample.reshape(ms, D_MODEL)], axis=0)
    xa_bf = xa.astype(BF16)
    w_out0_bf, w_out1_bf = w_out0.astype(BF16), w_out1.astype(BF16)
    w_o_mem_bf, w_down_bf = w_o_mem.astype(BF16), w_down.astype(BF16)
    p_nsa, p_win, p_conv, p_h, p_sc, p_mem = [], [], [], [], [], []
    s_nsa, s_win, s_conv, s_h, s_sc = [], [], [], [], []
    for l in range(DEPTH):
        if l % 2 == 0:
            e = l // 2
            lp = (lru_conv_w[e], lru_conv_b[e], lru_wx[e], lru_bx[e], lru_wa[e], lru_ba[e], lru_lambda[e])
            proj = matmul(xa_bf, w_in0, layer=e, n_cols=N_MAIN0)
            gt = matmul(xa_bf, _gate_weight(w_in0[e]))
            q_bf, kv4, kv_bf, kvw, kw_bf, gates, ch = nsa_prep(proj, gt, rope_cos, rope_sin)
            y_lru, h_p, h_s = rglru(proj, mp, state_lru_conv[e], state_lru_h[e], *lp, bs=bs, ts=ts)
            nq, nkv, nkw = NSA_HQ * NSA_HD, 4 * NSA_KVW, 2 * NSA_KVW
            n_pool = cache_nsa_kv.shape[1]
            o_nsa = nsa_prompt(q_bf, gates, ch, kv_bf, kw_bf, tp)
            o_nsa, win_s = nsa_sample(q_bf[mp:].reshape(bs, ts, nq), gates, kv4, kvw, mp,
                                      cache_nsa_kv.reshape(-1, NSA_HD), e * n_pool, page_table,
                                      state_nsa_win_kv.reshape(-1, NSA_HD), e * bs, state_nsa_win_kv.shape[2],
                                      o_nsa)
            mix = jnp.concatenate([y_lru, o_nsa], axis=1)
            xb_s = proj[mp:, :W_LRU].reshape(bs, ts, W_LRU)
            n_kv, n_kw = nkv // NSA_HD, nkw // NSA_HD
            p_nsa.append(kv4[:mp * n_kv].reshape(bp, tp, 4, NSA_HKV, NSA_HD))
            p_win.append(kvw[(mp - min(WINDOW, tp)) * n_kw:mp * n_kw].reshape(bp, -1, 2, NSA_HKV, NSA_HD))
            p_conv.append(proj[mp - (LRU_CONV - 1):mp, :W_LRU].reshape(bp, LRU_CONV - 1, W_LRU))
            p_h.append(h_p)
            s_nsa.append(kv4[mp * n_kv:].reshape(bs, ts, 4, NSA_HKV, NSA_HD))
            s_win.append(win_s.reshape(state_nsa_win_kv.shape[1:]))
            s_conv.append(xb_s[:, ts - (LRU_CONV - 1):])
            s_h.append(h_s.reshape(bs, ts, W_LRU)[:, ts - 1])
            w_out, l_out = w_out0_bf, e
        else:
            o = l // 2
            h3 = matmul(xa_bf, w_in1, layer=o)
            mix, z = short_conv(h3, mp, state_sconv[o], sconv_w[o], ts=ts)
            p_sc.append(z[mp - (SC_CONV - 1):mp].reshape(bp, SC_CONV - 1, D_MODEL))
            s_sc.append(z[mp:].reshape(bs, ts, D_MODEL)[:, ts - (SC_CONV - 1):])
            w_out, l_out = w_out1_bf, o
        xa, xa_bf = matmul_postnorm(mix, w_out, xa, ln_g[l, 0], ln_b[l, 0], layer=l_out)
        kv_mem_p = matmul(mem_prompt.reshape(bp * N_MEM, D_MODEL).astype(BF16), w_kv_mem, layer=l)
        p_mem.append(kv_mem_p.reshape(bp, N_MEM, 2, MEM_HEADS, MEM_HD))
        qm = matmul(xa_bf, w_q_mem, layer=l)
        om = cross_attention(qm, mp, kv_mem_p, cache_mem_kv, l, ts=ts)
        xa, xa_bf = matmul_postnorm(om, w_o_mem_bf, xa, ln_g[l, 1], ln_b[l, 1], layer=l)
        hf = matmul_swiglu(xa_bf, w_gu, layer=l)
        if l < DEPTH - 1:
            xa, xa_bf = matmul_postnorm(hf, w_down_bf, xa, ln_g[l, 2], ln_b[l, 2], layer=l)
        else:
            y_prompt, y_sample = matmul_postnorm(hf, w_down_bf, xa, ln_g[l, 2], ln_b[l, 2], layer=l,
                                                 split_rows=mp)
    return (y_prompt.reshape(bp, tp, D_MODEL), y_sample.reshape(bs, ts, D_MODEL),
            jnp.stack(p_nsa), jnp.stack(p_win), jnp.stack(p_conv), jnp.stack(p_h), jnp.stack(p_sc),
            jnp.stack(p_mem), jnp.stack(s_nsa), jnp.stack(s_win), jnp.stack(s_conv), jnp.stack(s_h),
            jnp.stack(s_sc))
```

```python
import functools

import numpy as np
import jax
import jax.numpy as jnp
from jax import lax
from jax.experimental import pallas as pl
from jax.experimental.pallas import tpu as pltpu

D_MODEL = 2048
SEQ = 8192
DEPTH = 2
DEC_BATCH = 128
DEC_SEQ = 8
PAGE_SIZE = 128

ALPHA = (2.0 * DEPTH) ** 0.25
LN_EPS = 1e-5
W_LRU = D_MODEL // 2
LRU_CONV = 4
LRU_BLOCKS = 8
LRU_BLOCK = W_LRU // LRU_BLOCKS
LRU_C = 8.0
NSA_HQ = 8
NSA_HKV = 2
NSA_HD = 128
NSA_G = NSA_HQ // NSA_HKV
NSA_KVW = NSA_HKV * NSA_HD
CMP_STRIDE = 16
CMP_LEN = 2 * CMP_STRIDE
SEL_BLOCK = 64
SEL_TOPK = 16
WINDOW = 512
Q_BLOCK = 128
ROPE_THETA = 10000.0
BIG = 1e6
NEG = -1e30
NSA_SCALE = NSA_HD ** -0.5
LOG2_E = 1.4426950408889634
EVEN_SPLITS = [W_LRU, 2 * W_LRU, 2 * W_LRU + NSA_HQ * NSA_HD,
               2 * W_LRU + NSA_HQ * NSA_HD + 4 * NSA_KVW,
               2 * W_LRU + NSA_HQ * NSA_HD + 6 * NSA_KVW]
N_MAIN0 = EVEN_SPLITS[-1]
N_GATE = 3 * NSA_HQ
SC_CONV = 3
N_MEM = 256
MEM_HEADS = 4
MEM_HD = D_MODEL // MEM_HEADS
D_FF = ((8 * D_MODEL + 3 * 256 - 1) // (3 * 256)) * 256

LANE = 128
VMEM_LIMIT = 56 * 1024 * 1024
SEL_TILE = 1024
MM_TILE_M = 1024
MM_TILE_N = 1024
MM_MAX_K = 2048
LN_TILE_M = 512
XATTN_SEQS = 2
NSA_SEQS = 2
LN_K_SPLIT = 4

F32 = jnp.float32
BF16 = jnp.bfloat16
I32 = jnp.int32


def _params(sem):
    return pltpu.CompilerParams(dimension_semantics=sem, vmem_limit_bytes=VMEM_LIMIT)


def _dot(a, b):
    return jnp.dot(a, b, preferred_element_type=F32)


def _as_bf16(x):
    return x if x.dtype == BF16 else x.astype(BF16)


def _dot_nt(a, b):
    return lax.dot_general(a, b, (((1,), (1,)), ((), ())), preferred_element_type=F32)


def _mm_kernel(x_ref, w_ref, o_ref):
    o_ref[...] = _dot(x_ref[...], _as_bf16(w_ref[...])).astype(o_ref.dtype)


def _pick_tile(n, prefs):
    for t in prefs:
        if n % t == 0:
            return t
    return n


def _weight_spec(w, layer, rows, cols, index_map):
    if w.ndim == 2:
        return pl.BlockSpec((rows, cols), index_map)
    return pl.BlockSpec((None, rows, cols), lambda *g: (layer,) + tuple(index_map(*g)))


def matmul(x, w, *, layer=0, n_cols=None, out_dtype=F32):
    m, kdim = x.shape
    n = w.shape[-1] if n_cols is None else n_cols
    assert x.dtype == BF16
    tm = _pick_tile(m, (MM_TILE_M,))
    tn = _pick_tile(n, (MM_TILE_N, MM_TILE_N // 2, MM_TILE_N // 4))
    return pl.pallas_call(
        _mm_kernel,
        grid=(m // tm, n // tn),
        in_specs=[pl.BlockSpec((tm, kdim), lambda i, j: (i, 0)),
                  _weight_spec(w, layer, kdim, tn, lambda i, j: (0, j))],
        out_specs=pl.BlockSpec((tm, tn), lambda i, j: (i, j)),
        out_shape=jax.ShapeDtypeStruct((m, n), out_dtype),
        compiler_params=_params(("parallel", "parallel")),
        name="matmul",
    )(x, w)


def _mm_ln_kernel(x_ref, w_ref, *rest, nk, n_first, n_res_first):
    if n_res_first is None:
        res_ref, g_ref, b_ref, o0_ref, o1_ref, *acc = rest
        residual = lambda: res_ref[...]
    else:
        res_a_ref, res_b_ref, g_ref, b_ref, o0_ref, o1_ref, *acc = rest
        residual = lambda: jnp.where(pl.program_id(0) < n_res_first, res_a_ref[...], res_b_ref[...])
    part = _dot(x_ref[...], w_ref[...])

    def finish(y):
        z = ALPHA * residual() + y
        mu = jnp.mean(z, axis=-1, keepdims=True)
        zc = z - mu
        var = jnp.mean(zc * zc, axis=-1, keepdims=True)
        out = zc * lax.rsqrt(var + LN_EPS) * g_ref[...] + b_ref[...]
        if n_first is None:
            o0_ref[...] = out
            o1_ref[...] = out.astype(BF16)
        else:
            i = pl.program_id(0)

            @pl.when(i < n_first)
            def _():
                o0_ref[...] = out

            @pl.when(i >= n_first)
            def _():
                o1_ref[...] = out

    if nk == 1:
        finish(part)
        return
    acc_ref, = acc
    k = pl.program_id(1)

    @pl.when(k == 0)
    def _():
        acc_ref[...] = part

    @pl.when(k > 0)
    def _():
        acc_ref[...] += part

    @pl.when(k == nk - 1)
    def _():
        finish(acc_ref[...])


def matmul_postnorm(x, w, res, g, b, *, layer=0, split_rows=None):
    m, kdim = x.shape
    n = w.shape[-1]
    assert x.dtype == BF16 and w.dtype == BF16
    tm = _pick_tile(m, (LN_TILE_M,))
    nk = 1 if kdim <= MM_MAX_K else LN_K_SPLIT
    assert kdim % (nk * LANE) == 0
    tk = kdim // nk
    row = pl.BlockSpec((tm, n), lambda i, k: (i, 0))
    if isinstance(res, tuple):
        res_a, res_b = res
        assert res_a.shape[0] % tm == 0 and res_b.shape[0] % tm == 0 and res_a.shape[0] + res_b.shape[0] == m
        n_res_first = res_a.shape[0] // tm
        res_args = [res_a, res_b]
        res_specs = [pl.BlockSpec((tm, n), lambda i, k: (jnp.minimum(i, n_res_first - 1), 0)),
                     pl.BlockSpec((tm, n), lambda i, k: (jnp.maximum(i - n_res_first, 0), 0))]
    else:
        assert res.shape == (m, n)
        n_res_first, res_args, res_specs = None, [res], [row]
    if split_rows is None:
        n_first = None
        out_specs = [row, row]
        out_shape = [jax.ShapeDtypeStruct((m, n), F32), jax.ShapeDtypeStruct((m, n), BF16)]
    else:
        assert split_rows % tm == 0 and 0 < split_rows < m
        n_first = split_rows // tm
        out_specs = [pl.BlockSpec((tm, n), lambda i, k: (jnp.minimum(i, n_first - 1), 0)),
                     pl.BlockSpec((tm, n), lambda i, k: (jnp.maximum(i - n_first, 0), 0))]
        out_shape = [jax.ShapeDtypeStruct((split_rows, n), F32), jax.ShapeDtypeStruct((m - split_rows, n), F32)]
    return pl.pallas_call(
        functools.partial(_mm_ln_kernel, nk=nk, n_first=n_first, n_res_first=n_res_first),
        grid=(m // tm, nk),
        in_specs=[pl.BlockSpec((tm, tk), lambda i, k: (i, k)),
                  _weight_spec(w, layer, tk, n, lambda i, k: (k, 0))] + res_specs
                 + [pl.BlockSpec((1, n), lambda i, k: (0, 0)),
                    pl.BlockSpec((1, n), lambda i, k: (0, 0))],
        out_specs=out_specs,
        out_shape=out_shape,
        scratch_shapes=[] if nk == 1 else [pltpu.VMEM((tm, n), F32)],
        compiler_params=_params(("arbitrary", "arbitrary")),
        name="matmul_postnorm",
    )(x, w, *res_args, g.reshape(1, n), b.reshape(1, n))


def _stack_kernel(a_ref, b_ref, o_ref, *, n_first):
    i = pl.program_id(0)

    @pl.when(i < n_first)
    def _():
        o_ref[...] = a_ref[...].astype(o_ref.dtype)

    @pl.when(i >= n_first)
    def _():
        o_ref[...] = b_ref[...].astype(o_ref.dtype)


def stack_rows_bf16(a, b):
    n = a.shape[1]
    tm = LN_TILE_M
    assert a.shape[0] % tm == 0 and b.shape[0] % tm == 0 and b.shape[1] == n
    n_first = a.shape[0] // tm
    m = a.shape[0] + b.shape[0]
    return pl.pallas_call(
        functools.partial(_stack_kernel, n_first=n_first),
        grid=(m // tm,),
        in_specs=[pl.BlockSpec((tm, n), lambda i: (jnp.minimum(i, n_first - 1), 0)),
                  pl.BlockSpec((tm, n), lambda i: (jnp.maximum(i - n_first, 0), 0))],
        out_specs=pl.BlockSpec((tm, n), lambda i: (i, 0)),
        out_shape=jax.ShapeDtypeStruct((m, n), BF16),
        compiler_params=_params(("parallel",)),
        name="stack_rows",
    )(a, b)


def _mm_swiglu_kernel(x_ref, wg_ref, wu_ref, o_ref):
    x = x_ref[...]
    g = _dot(x, _as_bf16(wg_ref[...]))
    u = _dot(x, _as_bf16(wu_ref[...]))
    o_ref[...] = (g * jax.nn.sigmoid(g) * u).astype(o_ref.dtype)


def matmul_swiglu(x, w_gu, *, layer=0):
    m, kdim = x.shape
    f = w_gu.shape[-1] // 2
    assert x.dtype == BF16
    tm = _pick_tile(m, (MM_TILE_M,))
    tn = _pick_tile(f, (MM_TILE_N // 2, MM_TILE_N // 4))
    nj = f // tn
    return pl.pallas_call(
        _mm_swiglu_kernel,
        grid=(m // tm, nj),
        in_specs=[pl.BlockSpec((tm, kdim), lambda i, j: (i, 0)),
                  _weight_spec(w_gu, layer, kdim, tn, lambda i, j: (0, j)),
                  _weight_spec(w_gu, layer, kdim, tn, lambda i, j: (0, j + nj))],
        out_specs=pl.BlockSpec((tm, tn), lambda i, j: (i, j)),
        out_shape=jax.ShapeDtypeStruct((m, f), BF16),
        compiler_params=_params(("parallel", "parallel")),
        name="matmul_swiglu",
    )(x, w_gu, w_gu)


def _rope_tables(pos):
    half = NSA_HD // 2
    inv = ROPE_THETA ** (-jnp.arange(half, dtype=F32) / half)
    ang = pos.astype(F32)[:, None] * inv[None, :]
    cos, sin = jnp.cos(ang), jnp.sin(ang)
    return jnp.concatenate([cos, cos], axis=-1), jnp.concatenate([-sin, sin], axis=-1)


def _nsa_prep_kernel(q_ref, kv_ref, kw_ref, gt_ref, cos_ref, sin_ref,
                     qbf_ref, kv_o_ref, kvbf_ref, kw_o_ref, kwbf_ref, g_o_ref, ch_ref):
    cos = cos_ref[...]
    sin = sin_ref[...]
    rows = q_ref.shape[0]

    def rot(x):
        return x * cos + pltpu.roll(x, NSA_HD // 2, axis=1) * sin

    for h in range(NSA_HQ):
        sl = slice(NSA_HD * h, NSA_HD * (h + 1))
        qbf_ref[:, sl] = rot(q_ref[:, sl]).astype(BF16)
    for c in range(4 * NSA_HKV):
        sl = slice(NSA_HD * c, NSA_HD * (c + 1))
        slot = c // NSA_HKV
        x = kv_ref[:, sl]
        if slot in (0, 2):
            x = rot(x)
        kv_o_ref[pl.ds(c, rows, stride=4 * NSA_HKV), :] = x
        kvbf_ref[:, sl] = x.astype(BF16)
        if slot < 2:
            ch_ref[:, sl] = jnp.sum(x.reshape(rows // CMP_STRIDE, CMP_STRIDE, NSA_HD), axis=1)
    for c in range(2 * NSA_HKV):
        sl = slice(NSA_HD * c, NSA_HD * (c + 1))
        x = kw_ref[:, sl]
        if c // NSA_HKV == 0:
            x = rot(x)
        kw_o_ref[pl.ds(c, rows, stride=2 * NSA_HKV), :] = x
        kwbf_ref[:, sl] = x.astype(BF16)
    g_o_ref[...] = jax.nn.sigmoid(gt_ref[...])


def nsa_prep(proj, gt, cos, sin, *, tm=512):
    m = proj.shape[0]
    assert m % tm == 0
    qc = EVEN_SPLITS[1] // (NSA_HQ * NSA_HD)
    kc = EVEN_SPLITS[2] // (4 * NSA_KVW)
    wc = EVEN_SPLITS[3] // (2 * NSA_KVW)
    nq, nkv, nkw, ng = NSA_HQ * NSA_HD, 4 * NSA_KVW, 2 * NSA_KVW, NSA_HKV * LANE
    row = lambda n: pl.BlockSpec((tm, n), lambda i: (i, 0))
    return pl.pallas_call(
        _nsa_prep_kernel,
        grid=(m // tm,),
        in_specs=[pl.BlockSpec((tm, nq), lambda i: (i, qc)),
                  pl.BlockSpec((tm, nkv), lambda i: (i, kc)),
                  pl.BlockSpec((tm, nkw), lambda i: (i, wc)),
                  row(ng), row(NSA_HD), row(NSA_HD)],
        out_specs=[row(nq), pl.BlockSpec((tm * nkv // NSA_HD, NSA_HD), lambda i: (i, 0)), row(nkv),
                   pl.BlockSpec((tm * nkw // NSA_HD, NSA_HD), lambda i: (i, 0)), row(nkw), row(ng),
                   pl.BlockSpec((tm // CMP_STRIDE, 2 * NSA_KVW), lambda i: (i, 0))],
        out_shape=[jax.ShapeDtypeStruct((m, nq), BF16),
                   jax.ShapeDtypeStruct((m * nkv // NSA_HD, NSA_HD), F32),
                   jax.ShapeDtypeStruct((m, nkv), BF16),
                   jax.ShapeDtypeStruct((m * nkw // NSA_HD, NSA_HD), F32),
                   jax.ShapeDtypeStruct((m, nkw), BF16),
                   jax.ShapeDtypeStruct((m, ng), F32),
                   jax.ShapeDtypeStruct((m // CMP_STRIDE, 2 * NSA_KVW), F32)],
        compiler_params=_params(("parallel",)),
        name="nsa_prep",
    )(proj, proj, proj, gt, cos, sin)


def _softmax_rows(s, mask):
    sm = jnp.where(mask, s, NEG)
    e = jnp.where(mask, jnp.exp(sm - jnp.max(sm, axis=-1, keepdims=True)), 0.0)
    return e * (1.0 / jnp.maximum(jnp.sum(e, axis=-1, keepdims=True), 1e-30))


def _block_importance(imp, pmat):
    hi = imp.astype(BF16)
    r1 = imp - hi.astype(F32)
    mid = r1.astype(BF16)
    lo = (r1 - mid.astype(F32)).astype(BF16)
    return _dot(hi, pmat) + _dot(mid, pmat) + _dot(lo, pmat)


def _select_blocks(imp_sel, posq):
    blk = lax.broadcasted_iota(I32, (1, LANE), 1)
    cur = posq // SEL_BLOCK
    forced = (blk == 0) | (blk == cur) | (blk == cur - 1)
    valid = blk * SEL_BLOCK <= posq
    score = jnp.where(valid, jnp.where(forced, BIG, imp_sel), -BIG)
    key = lax.bitcast_convert_type(score, I32)
    key = jnp.where(key < 0, key ^ 0x7FFFFFFF, key)
    key_m1 = key - 1
    cnt = jnp.zeros(score.shape, I32)
    for i in range(LANE):
        thr = jnp.where(blk > i, key_m1, key)
        cnt = cnt + (key[:, i:i + 1] > thr).astype(I32)
    return (cnt < SEL_TOPK).astype(F32)


def _pool_matrix(n_cmp):
    c = np.arange(n_cmp)[:, None]
    j = np.arange(LANE)[None, :]
    return jnp.asarray(((c // 4 == j).astype(np.float32) + ((c + 1) // 4 == j)), dtype=BF16)


def _select_blocks_wide(imp_sel, posq, n_live, key_ref):
    blk = lax.broadcasted_iota(I32, (1, LANE), 1)
    cur = posq // SEL_BLOCK
    forced = (blk == 0) | (blk == cur) | (blk == cur - 1)
    valid = blk * SEL_BLOCK <= posq
    score_t = jnp.where(valid, jnp.where(forced, BIG, imp_sel), -BIG).T
    key = lax.bitcast_convert_type(score_t, I32)
    key = jnp.where(key < 0, key ^ 0x7FFFFFFF, key)
    key_m1 = key - 1
    key_ref[...] = key
    blk_s = lax.broadcasted_iota(I32, (LANE, 1), 0)

    def body(c, cnt):
        base = pl.multiple_of(c * 8, 8)
        cand = key_ref[pl.ds(base, 8), :]
        for k in range(8):
            thr = jnp.where(blk_s > base + k, key_m1, key)
            cnt = cnt + (cand[k:k + 1, :] > thr).astype(I32)
        return cnt

    cnt = lax.fori_loop(0, n_live // 8, body, jnp.zeros((LANE, LANE), I32))
    return (cnt < SEL_TOPK).astype(F32)


def _nsa_prompt_kernel(*refs, n_qb):
    o_ref = refs[9]
    qb = pl.program_id(1)

    @pl.when(qb < n_qb)
    def _():
        _nsa_prompt_step(*refs)

    @pl.when(qb >= n_qb)
    def _():
        o_ref[...] = jnp.zeros(o_ref.shape, o_ref.dtype)


def _nsa_prompt_step(q_ref, g_ref, chk_ref, chv_ref, ks_ref, vs_ref, kw_ref, vw_ref, pmat_ref,
                     o_ref, kct_ref, vc_ref, key_ref):
    qb = pl.program_id(1)
    n_cmp = chk_ref.shape[0]

    @pl.when(qb == 0)
    def _():
        ck = chk_ref[...]
        kc = (ck + pltpu.roll(ck, n_cmp - 1, axis=0)) * (1.0 / CMP_LEN)
        kct_ref[...] = kc.T.astype(BF16)
        cv = chv_ref[...]
        vc_ref[...] = ((cv + pltpu.roll(cv, n_cmp - 1, axis=0)) * (1.0 / CMP_LEN)).astype(BF16)

    qblk = q_ref[...]
    q2 = jnp.concatenate([qblk[:, NSA_HD * g:NSA_HD * (g + 1)] for g in range(NSA_G)], axis=0)
    rows = NSA_G * Q_BLOCK
    posq = qb * Q_BLOCK + lax.broadcasted_iota(I32, (Q_BLOCK, 1), 0)
    posq4 = jnp.concatenate([posq] * NSA_G, axis=0)

    s = _dot(q2, kct_ref[...]) * NSA_SCALE
    cend = lax.broadcasted_iota(I32, (1, n_cmp), 1) * CMP_STRIDE + (CMP_LEN - 1)
    p = _softmax_rows(s, cend <= posq4)
    o_cmp = _dot(p.astype(BF16), vc_ref[...])
    imp = p[0:Q_BLOCK]
    for g in range(1, NSA_G):
        imp = imp + p[g * Q_BLOCK:(g + 1) * Q_BLOCK]
    n_live = ((qb + 1) * (Q_BLOCK // SEL_BLOCK) + 7) // 8 * 8
    sel_t = _select_blocks_wide(_block_importance(imp, pmat_ref[...]), posq, n_live, key_ref).astype(BF16)

    q2t = q2.astype(F32).T.astype(BF16)
    krow = lax.broadcasted_iota(I32, (SEL_TILE, 1), 0)
    bcol = lax.broadcasted_iota(I32, (1, LANE), 1)
    posq_l = qb * Q_BLOCK + bcol

    def tile_update(t, carry):
        m, l, acc = carry
        off = pl.multiple_of(t * SEL_TILE, SEL_TILE)
        k = ks_ref[pl.ds(off, SEL_TILE), :]
        v = vs_ref[pl.ds(off, SEL_TILE), :]
        st = _dot(k, q2t)
        expand = (krow // SEL_BLOCK + t * (SEL_TILE // SEL_BLOCK) == bcol).astype(BF16)
        chosen = _dot(expand, sel_t)
        msk = (chosen > 0.5) & (krow + t * SEL_TILE <= posq_l)
        bias = jnp.where(msk, 0.0, NEG)
        sm = st * (NSA_SCALE * LOG2_E) + jnp.concatenate([bias] * NSA_G, axis=1)
        m_new = jnp.maximum(m, jnp.max(sm, axis=0, keepdims=True))
        alpha = jnp.exp2(m - m_new)
        e = jnp.exp2(sm - m_new)
        l = alpha * l + jnp.sum(e, axis=0, keepdims=True)
        pv = lax.dot_general(v, e.astype(BF16), (((0,), (0,)), ((), ())), preferred_element_type=F32)
        return m_new, l, alpha * acc + pv

    m0 = jnp.full((1, rows), NEG, F32)
    l0 = jnp.zeros((1, rows), F32)
    n_tiles = (qb * Q_BLOCK) // SEL_TILE + 1
    _, l, acc = lax.fori_loop(0, n_tiles, tile_update, (m0, l0, jnp.zeros((NSA_HD, rows), F32)))
    o_sel = (acc / jnp.maximum(l, 1e-30)).T

    span = WINDOW + Q_BLOCK
    start = pl.multiple_of(jnp.maximum(qb - WINDOW // Q_BLOCK, 0) * Q_BLOCK, Q_BLOCK)
    kw = kw_ref[pl.ds(start, span), :]
    vw = vw_ref[pl.ds(start, span), :]
    s = _dot_nt(q2, kw) * NSA_SCALE
    dpos = posq4 - (start + lax.broadcasted_iota(I32, (1, span), 1))
    p = _softmax_rows(s, (dpos >= 0) & (dpos < WINDOW))
    o_win = _dot(p.astype(BF16), vw)

    gates = g_ref[...]
    for g in range(NSA_G):
        rs = slice(g * Q_BLOCK, (g + 1) * Q_BLOCK)
        o_ref[:, NSA_HD * g:NSA_HD * (g + 1)] = (gates[:, 3 * g:3 * g + 1] * o_cmp[rs]
                                                + gates[:, 3 * g + 1:3 * g + 2] * o_sel[rs]
                                                + gates[:, 3 * g + 2:3 * g + 3] * o_win[rs]).astype(o_ref.dtype)


def nsa_prompt(q_bf, gates, ch, kv_bf, kw_bf, t):
    m = q_bf.shape[0]
    n_qb = t // Q_BLOCK
    n_cmp = t // CMP_STRIDE
    assert m % Q_BLOCK == 0 and t % SEL_TILE == 0
    gw = NSA_G * NSA_HD
    res = lambda c0: pl.BlockSpec((t, NSA_HD), lambda h, i: (0, c0 + h))
    return pl.pallas_call(
        functools.partial(_nsa_prompt_kernel, n_qb=n_qb),
        grid=(NSA_HKV, m // Q_BLOCK),
        in_specs=[pl.BlockSpec((Q_BLOCK, gw), lambda h, i: (jnp.minimum(i, n_qb - 1), h)),
                  pl.BlockSpec((Q_BLOCK, LANE), lambda h, i: (jnp.minimum(i, n_qb - 1), h)),
                  pl.BlockSpec((n_cmp, NSA_HD), lambda h, i: (0, h)),
                  pl.BlockSpec((n_cmp, NSA_HD), lambda h, i: (0, NSA_HKV + h)),
                  res(2 * NSA_HKV), res(3 * NSA_HKV), res(0), res(NSA_HKV),
                  pl.BlockSpec((n_cmp, LANE), lambda h, i: (0, 0))],
        out_specs=pl.BlockSpec((Q_BLOCK, gw), lambda h, i: (i, h)),
        out_shape=jax.ShapeDtypeStruct((m, NSA_HQ * NSA_HD), BF16),
        scratch_shapes=[pltpu.VMEM((NSA_HD, n_cmp), BF16), pltpu.VMEM((n_cmp, NSA_HD), BF16),
                        pltpu.VMEM((LANE, LANE), I32)],
        compiler_params=_params(("parallel", "arbitrary")),
        name="nsa_prompt",
    )(q_bf, gates, ch, ch, kv_bf, kv_bf, kw_bf, kw_bf, _pool_matrix(n_cmp))


def _nsa_sample_kernel(pt_ref, q_ref, g_ref, kvn_ref, kwn_ref, win_ref, *rest, n_pages, past_len):
    n_seq, tq = q_ref.shape[:2]
    all_pages = rest[:n_seq * n_pages]
    pmat_ref, emat_ref, _, o_ref, nwin_ref = rest[n_seq * n_pages:]
    n_kv, n_kw = 4 * NSA_HKV, 2 * NSA_HKV
    w_buf = win_ref.shape[0] // (n_seq * n_kw)
    pad_rows = PAGE_SIZE - tq

    def rows_of(ref, c, n, per, base=0):
        return ref[pl.ds(base + c, n, stride=per), :]

    posq = past_len + lax.broadcasted_iota(I32, (tq, 1), 0)
    posq4 = jnp.concatenate([posq] * NSA_G, axis=0)
    zpad = jnp.zeros((pad_rows, NSA_HD), F32)

    def new_page(x):
        return jnp.concatenate([x, zpad], axis=0).astype(BF16)

    def chunk_sums(pg, c):
        x = rows_of(pg, c, PAGE_SIZE, n_kv)
        return jnp.sum(x.reshape(PAGE_SIZE // CMP_STRIDE, CMP_STRIDE, NSA_HD), axis=1)

    n_cmp = past_len // CMP_STRIDE
    crow = lax.broadcasted_iota(I32, (n_cmp, 1), 0)
    seq_out = []
    for sq in range(n_seq):
        pages = all_pages[sq * n_pages:(sq + 1) * n_pages]
        qf = q_ref[sq].astype(F32)
        gates = g_ref[sq * tq:(sq + 1) * tq, :]
        kv_base, kw_base, win_base = sq * tq * n_kv, sq * tq * n_kw, sq * w_buf * n_kw
        new_kv = lambda c: rows_of(kvn_ref, c, tq, n_kv, kv_base)
        new_kw = lambda c: rows_of(kwn_ref, c, tq, n_kw, kw_base)
        win = lambda c: rows_of(win_ref, c, w_buf, n_kw, win_base)

        def compressed(c):
            ch = jnp.concatenate([chunk_sums(pg, c) for pg in pages], axis=0)
            ch_new = jnp.sum(new_kv(c), axis=0, keepdims=True)
            nxt = jnp.where(crow == n_cmp - 1, ch_new, pltpu.roll(ch, n_cmp - 1, axis=0))
            return ((ch + nxt) * (1.0 / CMP_LEN)).astype(BF16)

        heads = []
        for h in range(NSA_HKV):
            q2 = jnp.concatenate([qf[:, NSA_HD * (NSA_G * h + g):NSA_HD * (NSA_G * h + g + 1)]
                                  for g in range(NSA_G)], axis=0).astype(BF16)
            col = lambda slot: slot * NSA_HKV + h

            kc = compressed(col(0))
            vc = compressed(col(1))
            s = _dot_nt(q2, kc) * NSA_SCALE
            cend = lax.broadcasted_iota(I32, (1, n_cmp), 1) * CMP_STRIDE + (CMP_LEN - 1)
            p = _softmax_rows(s, cend <= posq4)
            o_cmp = _dot(p.astype(BF16), vc)
            imp = p[0:tq]
            for g in range(1, NSA_G):
                imp = imp + p[g * tq:(g + 1) * tq]
            sel = _select_blocks(_block_importance(imp, pmat_ref[...]), posq)
            sel4 = jnp.concatenate([sel] * NSA_G, axis=0).astype(BF16)

            chosen = _dot(sel4, emat_ref[...])
            pieces = [_dot_nt(q2, rows_of(pg, col(2), PAGE_SIZE, n_kv).astype(BF16)) for pg in pages]
            pieces.append(_dot_nt(q2, new_page(new_kv(col(2)))))
            s = jnp.concatenate(pieces, axis=1) * NSA_SCALE
            n_key = s.shape[1]
            kpos = lax.broadcasted_iota(I32, (1, n_key), 1)
            p = _softmax_rows(s, (chosen > 0.5) & (kpos <= posq4)).astype(BF16)
            o_sel = _dot(p[:, past_len:], new_page(new_kv(col(3))))
            for i, pg in enumerate(pages):
                o_sel = o_sel + _dot(p[:, i * PAGE_SIZE:(i + 1) * PAGE_SIZE],
                                     rows_of(pg, col(3), PAGE_SIZE, n_kv).astype(BF16))

            kc_w, vc_w = h, NSA_HKV + h
            s = jnp.concatenate([_dot_nt(q2, win(kc_w).astype(BF16)),
                                 _dot_nt(q2, new_page(new_kw(kc_w)))], axis=1) * NSA_SCALE
            posw = (past_len - w_buf) + lax.broadcasted_iota(I32, (1, w_buf + PAGE_SIZE), 1)
            dpos = posq4 - posw
            p = _softmax_rows(s, (dpos >= 0) & (dpos < WINDOW) & (posw >= 0)).astype(BF16)
            o_win = _dot(p[:, :w_buf], win(vc_w).astype(BF16)) + _dot(p[:, w_buf:], new_page(new_kw(vc_w)))

            for g in range(NSA_G):
                rs = slice(g * tq, (g + 1) * tq)
                gc = h * LANE + 3 * g
                heads.append(gates[:, gc:gc + 1] * o_cmp[rs] + gates[:, gc + 1:gc + 2] * o_sel[rs]
                             + gates[:, gc + 2:gc + 3] * o_win[rs])
        seq_out.append(jnp.concatenate(heads, axis=1))

        rows = w_buf * n_kw
        nwin_ref[win_base:win_base + rows - tq * n_kw, :] = win_ref[win_base + tq * n_kw:win_base + rows, :]
        nwin_ref[win_base + rows - tq * n_kw:win_base + rows, :] = kwn_ref[kw_base:kw_base + tq * n_kw, :]

    o_ref[...] = jnp.concatenate(seq_out, axis=0).astype(o_ref.dtype)


def nsa_sample(q_bf, gates, kv_new, kw_new, mp, pool, page_base, page_table, win_all, win_base, w_buf, o_buf):
    bsz, tq, nq = q_bf.shape
    n_pages = page_table.shape[1]
    past_len = n_pages * PAGE_SIZE
    n_kv, n_kw = 4 * NSA_HKV, 2 * NSA_HKV
    ns = NSA_SEQS
    assert tq == 8 and past_len % SEL_BLOCK == 0 and mp % (ns * tq) == 0
    assert bsz % ns == 0 and win_base % ns == 0
    n_cmp = past_len // CMP_STRIDE
    n_key = past_len + PAGE_SIZE
    rb = mp // (ns * tq)
    emat = jnp.asarray(np.arange(LANE)[:, None] == (np.arange(n_key)[None, :] // SEL_BLOCK), dtype=BF16)
    seq = lambda r, n: pl.BlockSpec((ns * r, n), lambda b, pt: (rb + b, 0))
    page = lambda sq, p: pl.BlockSpec((PAGE_SIZE * n_kv, NSA_HD),
                                      lambda b, pt: (page_base + pt[(ns * b + sq) * n_pages + p], 0))
    const = lambda shape: pl.BlockSpec(shape, lambda b, pt: (0, 0))
    win_rows = ns * w_buf * n_kw
    return pl.pallas_call(
        functools.partial(_nsa_sample_kernel, n_pages=n_pages, past_len=past_len),
        grid_spec=pltpu.PrefetchScalarGridSpec(
            num_scalar_prefetch=1,
            grid=(bsz // ns,),
            in_specs=[pl.BlockSpec((ns, tq, nq), lambda b, pt: (b, 0, 0)),
                      seq(tq, gates.shape[1]), seq(tq * n_kv, NSA_HD), seq(tq * n_kw, NSA_HD),
                      pl.BlockSpec((win_rows, NSA_HD), lambda b, pt: (win_base // ns + b, 0))]
                     + [page(sq, p) for sq in range(ns) for p in range(n_pages)]
                     + [const((n_cmp, LANE)), const((LANE, n_key)), pl.BlockSpec(memory_space=pl.ANY)],
            out_specs=[seq(tq, nq), pl.BlockSpec((win_rows, NSA_HD), lambda b, pt: (b, 0))]),
        out_shape=[jax.ShapeDtypeStruct(o_buf.shape, o_buf.dtype),
                   jax.ShapeDtypeStruct((bsz * w_buf * n_kw, NSA_HD), F32)],
        input_output_aliases={8 + ns * n_pages: 0},
        compiler_params=_params(("parallel",)),
        name="nsa_sample",
    )(page_table.reshape(-1), q_bf, gates, kv_new, kw_new, win_all, *([pool] * (ns * n_pages)),
      _pool_matrix(n_cmp), emat, o_buf)


def _shift_rows(x, s):
    return pltpu.roll(x, s % x.shape[0], axis=0)


def _lru_recurrence_inputs(xc, wx_ref, bx_ref, wa_ref, ba_ref, lam_ref):
    xb = xc.astype(BF16)
    i_g = jax.nn.sigmoid(_dot(xb, wx_ref[0].astype(BF16)) + bx_ref[...])
    r_g = jax.nn.sigmoid(_dot(xb, wa_ref[0].astype(BF16)) + ba_ref[...])
    z = -lam_ref[...]
    softplus = jnp.maximum(z, 0.0) + jnp.log1p(jnp.exp(-jnp.abs(z)))
    log_a = -LRU_C * r_g * softplus
    a = jnp.exp(log_a)
    th = jnp.tanh(log_a)
    u = jnp.sqrt(-2.0 * th / (1.0 - th)) * (i_g * xc)
    return a, u


def _scan_rows(a, u, tpos, length):
    d = 1
    while d < length:
        keep = tpos >= d
        a_prev = jnp.where(keep, _shift_rows(a, d), 1.0)
        u_prev = jnp.where(keep, _shift_rows(u, d), 0.0)
        u = a * u_prev + u
        a = a * a_prev
        d *= 2
    return a, u


def _lru_prompt_kernel(*refs, n_tiles):
    y_ref = refs[9]
    tt = pl.program_id(1)

    @pl.when(tt < n_tiles)
    def _():
        _lru_prompt_step(*refs)

    @pl.when(tt >= n_tiles)
    def _():
        y_ref[...] = jnp.zeros(y_ref.shape, y_ref.dtype)


def _lru_prompt_step(x_ref, g_ref, cw_ref, cb_ref, wx_ref, bx_ref, wa_ref, ba_ref, lam_ref,
                     y_ref, hl_ref, tail_ref, hs_ref):
    tt = pl.program_id(1)
    rows = x_ref.shape[0]

    @pl.when(tt == 0)
    def _():
        tail_ref[...] = jnp.zeros(tail_ref.shape, F32)
        hs_ref[...] = jnp.zeros(hs_ref.shape, F32)

    x = x_ref[...]
    w = cw_ref[...]
    tail = tail_ref[...]
    row8 = lax.broadcasted_iota(I32, (8, LRU_BLOCK), 0)
    acc = None
    for i in range(LRU_CONV):
        s = LRU_CONV - 1 - i
        if s == 0:
            xs = x
        else:
            xs = _shift_rows(x, s)
            head = jnp.where(row8 < s, _shift_rows(tail, s), xs[0:8])
            xs = jnp.concatenate([head, xs[8:]], axis=0)
        term = w[i:i + 1] * xs
        acc = term if acc is None else acc + term
    xc = acc + cb_ref[...]
    tail_ref[...] = x[rows - 8:rows]

    a, u = _lru_recurrence_inputs(xc, wx_ref, bx_ref, wa_ref, ba_ref, lam_ref)
    tpos = lax.broadcasted_iota(I32, (rows, 1), 0)
    a, u = _scan_rows(a, u, tpos, rows)
    h = u + a * hs_ref[7:8, :]
    hs_ref[...] = h[rows - 8:rows]
    hl_ref[...] = h[rows - 1:rows]
    y_ref[...] = (h * jax.nn.gelu(g_ref[...])).astype(y_ref.dtype)


def _lru_sample_kernel(x_ref, g_ref, buf_ref, h0_ref, cw_ref, cb_ref, wx_ref, bx_ref, wa_ref, ba_ref,
                       lam_ref, mix_ref, y_ref, h_ref, *, tq):
    del mix_ref
    rows = x_ref.shape[0]
    x = x_ref[...]
    w = cw_ref[...]
    buf = buf_ref[...]
    tpos = lax.broadcasted_iota(I32, (rows, 1), 0) % tq
    acc = None
    for i in range(LRU_CONV):
        s = LRU_CONV - 1 - i
        xs = x if s == 0 else jnp.where(tpos >= s, _shift_rows(x, s), _shift_rows(buf, -i))
        term = w[i:i + 1] * xs
        acc = term if acc is None else acc + term
    xc = acc + cb_ref[...]
    a, u = _lru_recurrence_inputs(xc, wx_ref, bx_ref, wa_ref, ba_ref, lam_ref)
    a, u = _scan_rows(a, u, tpos, tq)
    h = u + a * h0_ref[...]
    h_ref[...] = h
    y_ref[...] = (h * jax.nn.gelu(g_ref[...])).astype(y_ref.dtype)


def rglru(proj, mp, conv_state, h0, conv_w, conv_b, wx, bx, wa, ba, lam, *, bs, ts, tt=512):
    m = proj.shape[0]
    ms = bs * ts
    nb = LRU_BLOCKS
    assert mp % tt == 0 and mp % ms == 0 and ts == 8
    vec = lambda v: v.reshape(1, W_LRU)
    cspec = lambda shape, im: pl.BlockSpec(shape, im)
    par_p = [cspec((LRU_CONV, LRU_BLOCK), lambda c, t: (0, c)), cspec((1, LRU_BLOCK), lambda c, t: (0, c)),
             cspec((1, LRU_BLOCK, LRU_BLOCK), lambda c, t: (c, 0, 0)), cspec((1, LRU_BLOCK), lambda c, t: (0, c)),
             cspec((1, LRU_BLOCK, LRU_BLOCK), lambda c, t: (c, 0, 0)), cspec((1, LRU_BLOCK), lambda c, t: (0, c)),
             cspec((1, LRU_BLOCK), lambda c, t: (0, c))]
    params = (conv_w, vec(conv_b), wx, vec(bx), wa, vec(ba), vec(lam))
    npt = mp // tt
    assert m % tt == 0
    y, h_last = pl.pallas_call(
        functools.partial(_lru_prompt_kernel, n_tiles=npt),
        grid=(nb, m // tt),
        in_specs=[pl.BlockSpec((tt, LRU_BLOCK), lambda c, t: (jnp.minimum(t, npt - 1), c)),
                  pl.BlockSpec((tt, LRU_BLOCK), lambda c, t: (jnp.minimum(t, npt - 1), nb + c))] + par_p,
        out_specs=[pl.BlockSpec((tt, LRU_BLOCK), lambda c, t: (t, c)),
                   pl.BlockSpec((1, LRU_BLOCK), lambda c, t: (0, c))],
        out_shape=[jax.ShapeDtypeStruct((m, W_LRU), BF16), jax.ShapeDtypeStruct((1, W_LRU), F32)],
        scratch_shapes=[pltpu.VMEM((8, LRU_BLOCK), F32), pltpu.VMEM((8, LRU_BLOCK), F32)],
        compiler_params=_params(("parallel", "arbitrary")),
        name="lru_prompt",
    )(proj, proj, *params)

    buf = jnp.pad(conv_state, ((0, 0), (0, ts - (LRU_CONV - 1)), (0, 0))).reshape(ms, W_LRU)
    h0r = jnp.repeat(h0, ts, axis=0)
    rb = mp // ms
    par_s = [cspec((LRU_CONV, LRU_BLOCK), lambda c: (0, c)), cspec((1, LRU_BLOCK), lambda c: (0, c)),
             cspec((1, LRU_BLOCK, LRU_BLOCK), lambda c: (c, 0, 0)), cspec((1, LRU_BLOCK), lambda c: (0, c)),
             cspec((1, LRU_BLOCK, LRU_BLOCK), lambda c: (c, 0, 0)), cspec((1, LRU_BLOCK), lambda c: (0, c)),
             cspec((1, LRU_BLOCK), lambda c: (0, c))]
    y, h_s = pl.pallas_call(
        functools.partial(_lru_sample_kernel, tq=ts),
        grid=(nb,),
        in_specs=[pl.BlockSpec((ms, LRU_BLOCK), lambda c: (rb, c)),
                  pl.BlockSpec((ms, LRU_BLOCK), lambda c: (rb, nb + c)),
                  pl.BlockSpec((ms, LRU_BLOCK), lambda c: (0, c)),
                  pl.BlockSpec((ms, LRU_BLOCK), lambda c: (0, c))] + par_s
                 + [pl.BlockSpec(memory_space=pl.ANY)],
        out_specs=[pl.BlockSpec((ms, LRU_BLOCK), lambda c: (rb, c)),
                   pl.BlockSpec((ms, LRU_BLOCK), lambda c: (0, c))],
        out_shape=[jax.ShapeDtypeStruct((m, W_LRU), BF16), jax.ShapeDtypeStruct((ms, W_LRU), F32)],
        input_output_aliases={4 + len(par_s): 0},
        compiler_params=_params(("parallel",)),
        name="lru_sample",
    )(proj, proj, buf, h0r, *params, y)
    return y, h_last, h_s


def _sconv_kernel(h_ref, prev_ref, buf_ref, w_ref, o_ref, z_ref, *, n_prompt_tiles, tq):
    i = pl.program_id(0)
    rows = h_ref.shape[0]
    n = o_ref.shape[1]
    bg = h_ref[:, 0:n]
    z = h_ref[:, n:2 * n] * h_ref[:, 2 * n:3 * n]
    z_ref[...] = z
    w = w_ref[...]

    def finish(fix):
        acc = None
        for k in range(SC_CONV):
            s = SC_CONV - 1 - k
            zs = z if s == 0 else fix(_shift_rows(z, s), s, k)
            term = w[k:k + 1] * zs
            acc = term if acc is None else acc + term
        o_ref[...] = (bg * acc).astype(o_ref.dtype)

    @pl.when(i < n_prompt_tiles)
    def _():
        pz = prev_ref[:, n:2 * n] * prev_ref[:, 2 * n:3 * n]
        pz = jnp.where(i > 0, pz, 0.0)
        row8 = lax.broadcasted_iota(I32, (8, 1), 0)

        def fix(zs, s, k):
            head = jnp.where(row8 < s, _shift_rows(pz, s), zs[0:8])
            return jnp.concatenate([head, zs[8:]], axis=0)

        finish(fix)

    @pl.when(i >= n_prompt_tiles)
    def _():
        tpos = lax.broadcasted_iota(I32, (rows, 1), 0) % tq
        buf = buf_ref[...]
        finish(lambda zs, s, k: jnp.where(tpos >= s, zs, _shift_rows(buf, -k)))


def short_conv(h3, mp, conv_state, conv_w, *, ts, tm=256):
    m = h3.shape[0]
    n = h3.shape[1] // 3
    ms = m - mp
    assert mp % tm == 0 and ms % tm == 0 and ts == 8
    npt = mp // tm
    buf = jnp.pad(conv_state, ((0, 0), (0, ts - (SC_CONV - 1)), (0, 0))).reshape(ms, n)
    return pl.pallas_call(
        functools.partial(_sconv_kernel, n_prompt_tiles=npt, tq=ts),
        grid=(m // tm,),
        in_specs=[pl.BlockSpec((tm, 3 * n), lambda i: (i, 0)),
                  pl.BlockSpec((8, 3 * n), lambda i: (jnp.maximum(jnp.minimum(i, npt) * (tm // 8) - 1, 0), 0)),
                  pl.BlockSpec((tm, n), lambda i: (jnp.maximum(i - npt, 0), 0)),
                  pl.BlockSpec((SC_CONV, n), lambda i: (0, 0))],
        out_specs=[pl.BlockSpec((tm, n), lambda i: (i, 0)), pl.BlockSpec((tm, n), lambda i: (i, 0))],
        out_shape=[jax.ShapeDtypeStruct((m, n), BF16), jax.ShapeDtypeStruct((m, n), F32)],
        compiler_params=_params(("parallel",)),
        name="short_conv",
    )(h3, h3, buf, conv_w)


def _softmax_plain(s):
    e = jnp.exp(s - jnp.max(s, axis=-1, keepdims=True))
    return e * (1.0 / jnp.sum(e, axis=-1, keepdims=True))


def _xattn_heads(q, kv, o_ref):
    width = MEM_HEADS * MEM_HD
    for h in range(MEM_HEADS):
        sl = slice(h * MEM_HD, (h + 1) * MEM_HD)
        kh = kv[:, sl].astype(BF16)
        vh = kv[:, width + h * MEM_HD:width + (h + 1) * MEM_HD].astype(BF16)
        p = _softmax_plain(_dot_nt(q[:, sl].astype(BF16), kh) * (MEM_HD ** -0.5))
        o_ref[:, sl] = _dot(p.astype(BF16), vh).astype(o_ref.dtype)


def _xattn_prompt_kernel(q_ref, kv_ref, o_ref, *, n_tiles):
    i = pl.program_id(0)

    @pl.when(i < n_tiles)
    def _():
        _xattn_heads(q_ref[...], kv_ref[...], o_ref)

    @pl.when(i >= n_tiles)
    def _():
        o_ref[...] = jnp.zeros(o_ref.shape, o_ref.dtype)


def _xattn_sample_kernel(q_ref, k_ref, v_ref, buf_ref, o_ref, *, tq):
    del buf_ref
    nc = MEM_HD // LANE
    n_seq, n_mem = k_ref.shape[:2]
    rows, cols = MEM_HEADS * tq, MEM_HEADS * n_mem
    row_head = lax.broadcasted_iota(I32, (rows, 1), 0) // tq
    col_head = lax.broadcasted_iota(I32, (1, cols), 1) % MEM_HEADS
    outs = []
    for i in range(n_seq):
        q = q_ref[i * tq:(i + 1) * tq, :]

        def slab(ref, c):
            return ref[i, :, :, c * LANE:(c + 1) * LANE].reshape(cols, LANE).astype(BF16)

        s = None
        for c in range(nc):
            qc = jnp.concatenate([q[:, h * MEM_HD + c * LANE:h * MEM_HD + (c + 1) * LANE]
                                  for h in range(MEM_HEADS)], axis=0).astype(BF16)
            part = _dot_nt(qc, slab(k_ref, c))
            s = part if s is None else s + part
        p = _softmax_rows(s * (MEM_HD ** -0.5), row_head == col_head).astype(BF16)
        oc = [_dot(p, slab(v_ref, c)) for c in range(nc)]
        outs.append(jnp.concatenate([oc[c][h * tq:(h + 1) * tq] for h in range(MEM_HEADS) for c in range(nc)],
                                    axis=1))
    o_ref[...] = jnp.concatenate(outs, axis=0).astype(o_ref.dtype)


def cross_attention(qm, mp, kv_prompt, kv_cache, layer, *, ts, tm=512):
    m, width = qm.shape
    bs = (m - mp) // ts
    assert mp % tm == 0 and mp % ts == 0 and m % tm == 0
    npt = mp // tm
    o = pl.pallas_call(
        functools.partial(_xattn_prompt_kernel, n_tiles=npt),
        grid=(m // tm,),
        in_specs=[pl.BlockSpec((tm, width), lambda i: (jnp.minimum(i, npt - 1), 0)),
                  pl.BlockSpec(kv_prompt.shape, lambda i: (0, 0))],
        out_specs=pl.BlockSpec((tm, width), lambda i: (i, 0)),
        out_shape=jax.ShapeDtypeStruct((m, width), BF16),
        compiler_params=_params(("parallel",)),
        name="xattn_prompt",
    )(qm, kv_prompt)
    rows = XATTN_SEQS * ts
    assert bs % XATTN_SEQS == 0 and mp % rows == 0
    rb = mp // rows
    slabs = [pl.BlockSpec((None, XATTN_SEQS, N_MEM, None, MEM_HEADS, MEM_HD),
                          lambda b, kv=kv: (layer, b, 0, kv, 0, 0)) for kv in range(2)]
    return pl.pallas_call(
        functools.partial(_xattn_sample_kernel, tq=ts),
        grid=(bs // XATTN_SEQS,),
        in_specs=[pl.BlockSpec((rows, width), lambda b: (rb + b, 0))] + slabs
                 + [pl.BlockSpec(memory_space=pl.ANY)],
        out_specs=pl.BlockSpec((rows, width), lambda b: (rb + b, 0)),
        out_shape=jax.ShapeDtypeStruct((m, width), BF16),
        input_output_aliases={1 + len(slabs): 0},
        compiler_params=_params(("parallel",)),
        name="xattn_sample",
    )(qm, *([kv_cache] * len(slabs)), o)


def _gate_weight(w_in):
    per = N_GATE // NSA_HKV
    parts = [jnp.pad(w_in[:, N_MAIN0 + h * per:N_MAIN0 + (h + 1) * per], ((0, 0), (0, LANE - per)))
             for h in range(NSA_HKV)]
    return jnp.concatenate(parts, axis=1)


def kernel(x_prompt, x_sample, cache_nsa_kv, state_nsa_win_kv, state_lru_conv, state_lru_h, state_sconv, cache_mem_kv, page_table, mem_prompt, w_in0, lru_conv_w, lru_conv_b, lru_wx, lru_bx, lru_wa, lru_ba, lru_lambda, w_out0, w_in1, sconv_w, w_out1, w_q_mem, w_kv_mem, w_o_mem, w_gu, w_down, ln_g, ln_b):
    bp, tp = x_prompt.shape[:2]
    bs, ts = x_sample.shape[:2]
    assert bp == 1
    mp, ms = bp * tp, bs * ts
    past_len = page_table.shape[1] * PAGE_SIZE
    pos_all = jnp.concatenate([jnp.arange(tp, dtype=I32),
                               jnp.tile(past_len + jnp.arange(ts, dtype=I32), bs)])
    rope_cos, rope_sin = _rope_tables(pos_all)
    xa = (x_prompt.reshape(mp, D_MODEL), x_sample.reshape(ms, D_MODEL))
    xa_bf = stack_rows_bf16(*xa)
    w_out0_bf, w_out1_bf = w_out0.astype(BF16), w_out1.astype(BF16)
    w_o_mem_bf, w_down_bf = w_o_mem.astype(BF16), w_down.astype(BF16)
    p_nsa, p_win, p_conv, p_h, p_sc, p_mem = [], [], [], [], [], []
    s_nsa, s_win, s_conv, s_h, s_sc = [], [], [], [], []
    for l in range(DEPTH):
        if l % 2 == 0:
            e = l // 2
            lp = (lru_conv_w[e], lru_conv_b[e], lru_wx[e], lru_bx[e], lru_wa[e], lru_ba[e], lru_lambda[e])
            proj = matmul(xa_bf, w_in0, layer=e, n_cols=N_MAIN0)
            gt = matmul(xa_bf, _gate_weight(w_in0[e]))
            q_bf, kv4, kv_bf, kvw, kw_bf, gates, ch = nsa_prep(proj, gt, rope_cos, rope_sin)
            y_lru, h_p, h_s = rglru(proj, mp, state_lru_conv[e], state_lru_h[e], *lp, bs=bs, ts=ts)
            nq, nkv, nkw = NSA_HQ * NSA_HD, 4 * NSA_KVW, 2 * NSA_KVW
            n_pool = cache_nsa_kv.shape[1]
            o_nsa = nsa_prompt(q_bf, gates, ch, kv_bf, kw_bf, tp)
            o_nsa, win_s = nsa_sample(q_bf[mp:].reshape(bs, ts, nq), gates, kv4, kvw, mp,
                                      cache_nsa_kv.reshape(-1, NSA_HD), e * n_pool, page_table,
                                      state_nsa_win_kv.reshape(-1, NSA_HD), e * bs, state_nsa_win_kv.shape[2],
                                      o_nsa)
            mix = jnp.concatenate([y_lru, o_nsa], axis=1)
            xb_s = proj[mp:, :W_LRU].reshape(bs, ts, W_LRU)
            n_kv, n_kw = nkv // NSA_HD, nkw // NSA_HD
            p_nsa.append(kv4[:mp * n_kv].reshape(bp, tp, 4, NSA_HKV, NSA_HD))
            p_win.append(kvw[(mp - min(WINDOW, tp)) * n_kw:mp * n_kw].reshape(bp, -1, 2, NSA_HKV, NSA_HD))
            p_conv.append(proj[mp - (LRU_CONV - 1):mp, :W_LRU].reshape(bp, LRU_CONV - 1, W_LRU))
            p_h.append(h_p)
            s_nsa.append(kv4[mp * n_kv:].reshape(bs, ts, 4, NSA_HKV, NSA_HD))
            s_win.append(win_s.reshape(state_nsa_win_kv.shape[1:]))
            s_conv.append(xb_s[:, ts - (LRU_CONV - 1):])
            s_h.append(h_s.reshape(bs, ts, W_LRU)[:, ts - 1])
            w_out, l_out = w_out0_bf, e
        else:
            o = l // 2
            h3 = matmul(xa_bf, w_in1, layer=o)
            mix, z = short_conv(h3, mp, state_sconv[o], sconv_w[o], ts=ts)
            p_sc.append(z[mp - (SC_CONV - 1):mp].reshape(bp, SC_CONV - 1, D_MODEL))
            s_sc.append(z[mp:].reshape(bs, ts, D_MODEL)[:, ts - (SC_CONV - 1):])
            w_out, l_out = w_out1_bf, o
        xa, xa_bf = matmul_postnorm(mix, w_out, xa, ln_g[l, 0], ln_b[l, 0], layer=l_out)
        kv_mem_p = matmul(mem_prompt.reshape(bp * N_MEM, D_MODEL).astype(BF16), w_kv_mem, layer=l)
        p_mem.append(kv_mem_p.reshape(bp, N_MEM, 2, MEM_HEADS, MEM_HD))
        qm = matmul(xa_bf, w_q_mem, layer=l)
        om = cross_attention(qm, mp, kv_mem_p, cache_mem_kv, l, ts=ts)
        xa, xa_bf = matmul_postnorm(om, w_o_mem_bf, xa, ln_g[l, 1], ln_b[l, 1], layer=l)
        hf = matmul_swiglu(xa_bf, w_gu, layer=l)
        if l < DEPTH - 1:
            xa, xa_bf = matmul_postnorm(hf, w_down_bf, xa, ln_g[l, 2], ln_b[l, 2], layer=l)
        else:
            y_prompt, y_sample = matmul_postnorm(hf, w_down_bf, xa, ln_g[l, 2], ln_b[l, 2], layer=l,
                                                 split_rows=mp)
    return (y_prompt.reshape(bp, tp, D_MODEL), y_sample.reshape(bs, ts, D_MODEL),
            jnp.stack(p_nsa), jnp.stack(p_win), jnp.stack(p_conv), jnp.stack(p_h), jnp.stack(p_sc),
            jnp.stack(p_mem), jnp.stack(s_nsa), jnp.stack(s_win), jnp.stack(s_conv), jnp.stack(s_h),
            jnp.stack(s_sc))
```
